```python
import math
import jax, jax.numpy as jnp
from jax import lax
import numpy as np

D_MODEL = 1024
BATCH = 4
SEQ = 4096
DEPTH = 2
DEC_BATCH = 128
DEC_SEQ = 8
PAST_LEN = 2048
PAGE_SIZE = 128

H_A = 8
G_A = 2
HG_A = H_A // G_A
DH = 64
L_CMP = 32
L_SEL = 64
N_SEL = 16
WINDOW = 512
H_B = 8
N_BUCKETS = 32
MAX_DISTANCE = 128
PEER_HEADS = 8
N_KEYS = 128
N_EXPERTS = N_KEYS * N_KEYS
PEER_TOPK = 16
D_KEY = 256
D_HALF = D_KEY // 2
PEER_TOKEN_BLOCK = 128
D_PLE = 256

Q_BLOCK = 128
EPS = 1e-6
NEG = -1e30
FORCE = 1e4
SCALE = DH ** -0.5
W_A = H_A * DH
W_B = H_B * DH
SPLIT_SIZES = (W_A, 2 * G_A * DH, 2 * G_A * DH, 2 * G_A * DH, 3 * H_A,
               W_B, W_B, W_B, H_B, D_MODEL, D_MODEL)
N_IN = sum(SPLIT_SIZES)

kernel_name = "nsa_fox_peer_hybrid_step"


def rms_norm(x, g):
    x32 = x.astype(jnp.float32)
    y = x32 * lax.rsqrt(jnp.mean(x32 * x32, axis=-1, keepdims=True) + EPS)
    return (y * g.astype(jnp.float32)).astype(x.dtype)


def t5_bucket(dist):
    n = jnp.maximum(dist, 0)
    max_exact = N_BUCKETS // 2
    nf = jnp.maximum(n, 1).astype(jnp.float32)
    large = max_exact + (jnp.log(nf / max_exact) / math.log(MAX_DISTANCE / max_exact)
                         * (N_BUCKETS - max_exact)).astype(jnp.int32)
    large = jnp.minimum(large, N_BUCKETS - 1)
    return jnp.where(n < max_exact, n, large)


def rel_bias_grid(dist, table):
    b = table[t5_bucket(dist)].astype(jnp.float32)
    return jnp.moveaxis(b.reshape(*dist.shape, G_A, HG_A), (-2, -1), (0, 1))


def over_query_blocks(fn, q_pos, *qs):
    tq = q_pos.shape[0]
    if tq <= Q_BLOCK or tq % Q_BLOCK:
        return fn(q_pos, *qs)
    nb = tq // Q_BLOCK
    split = lambda a: jnp.moveaxis(a.reshape(a.shape[0], nb, Q_BLOCK, *a.shape[2:]), 1, 0)
    out = lax.map(lambda args: fn(*args), (q_pos.reshape(nb, Q_BLOCK),) + tuple(split(a) for a in qs))
    merge = lambda a: jnp.moveaxis(a, 0, 1).reshape(a.shape[1], tq, *a.shape[3:])
    return jax.tree_util.tree_map(merge, out)


def masked_attend(q, k, v, bias, mask):
    s = jnp.einsum('btgrd,bsgd->bgrts', q, k).astype(jnp.float32) * SCALE + bias
    p = jnp.where(mask, jax.nn.softmax(jnp.where(mask, s, NEG), axis=-1), 0.0)
    return jnp.einsum('bgrts,bsgd->btgrd', p.astype(v.dtype), v)


def window_attend(q, q_pos, k, v, k_pos, table):
    b, tq = q.shape[:2]
    dist = q_pos[:, None] - k_pos[None, :]
    mask = (dist >= 0) & (dist < WINDOW) & (k_pos[None, :] >= 0)
    o = masked_attend(q.reshape(b, tq, G_A, HG_A, DH), k, v, rel_bias_grid(dist, table), mask)
    return o.reshape(b, tq, H_A, DH)


def window_prompt(q, kw, vw, table):
    t = q.shape[1]
    padf = lambda a: jnp.pad(a, ((0, 0), (WINDOW, 0), (0, 0), (0, 0)))
    kp, vp = padf(kw), padf(vw)

    def blk(q_pos, qb):
        span = WINDOW + q_pos.shape[0]
        start = q_pos[0]
        kb = lax.dynamic_slice_in_dim(kp, start, span, axis=1)
        vb = lax.dynamic_slice_in_dim(vp, start, span, axis=1)
        k_pos = start - WINDOW + jnp.arange(span)
        return window_attend(qb, q_pos, kb, vb, k_pos, table)

    return over_query_blocks(blk, jnp.arange(t), q)


def nsa_cmp_sel(q, q_pos, kc, vc, ks, vs, w_ck, w_cv, pe, table):
    b, tk = kc.shape[:2]
    pad = (-tk) % L_SEL
    padt = lambda a: jnp.pad(a, ((0, 0), (0, pad), (0, 0), (0, 0)))
    kc, vc, ks, vs = padt(kc), padt(vc), padt(ks), padt(vs)
    tp = tk + pad
    nb_c, nb_s = tp // L_CMP, tp // L_SEL
    n_sel = min(N_SEL, nb_s)
    blocks = lambda a: a.reshape(b, nb_c, L_CMP, G_A, DH) + pe[:, None, :].astype(a.dtype)
    k_cmp = jnp.einsum('bnlgd,lde->bnge', blocks(kc), w_ck)
    v_cmp = jnp.einsum('bnlgd,lde->bnge', blocks(vc), w_cv)
    end_pos = jnp.arange(nb_c) * L_CMP + (L_CMP - 1)
    ks_b = jnp.moveaxis(ks.reshape(b, nb_s, L_SEL, G_A, DH), 3, 1)
    vs_b = jnp.moveaxis(vs.reshape(b, nb_s, L_SEL, G_A, DH), 3, 1)
    b_ix = jnp.arange(b)[:, None, None, None]
    g_ix = jnp.arange(G_A)[None, :, None, None]
    table_g = table.reshape(N_BUCKETS, G_A, HG_A)

    def blk(q_pos, qb):
        tb = q_pos.shape[0]
        qg = qb.reshape(b, tb, G_A, HG_A, DH)
        dist_c = q_pos[:, None] - end_pos[None, :]
        mask_c = dist_c >= 0
        s = jnp.einsum('btgrd,bngd->bgrtn', qg, k_cmp).astype(jnp.float32) * SCALE + rel_bias_grid(dist_c, table)
        p = jnp.where(mask_c, jax.nn.softmax(jnp.where(mask_c, s, NEG), axis=-1), 0.0)
        o_cmp = jnp.einsum('bgrtn,bngd->btgrd', p.astype(v_cmp.dtype), v_cmp)
        imp = p.sum(2).reshape(b, G_A, tb, nb_s, L_SEL // L_CMP).sum(-1)
        cur = (q_pos // L_SEL)[:, None]
        bidx = jnp.arange(nb_s)[None, :]
        imp = jnp.where((bidx == 0) | (bidx == cur), FORCE, imp)
        imp = jnp.where(bidx <= cur, imp, NEG)
        _, sel = lax.top_k(imp, n_sel)
        kg = ks_b[b_ix, g_ix, sel]
        vg = vs_b[b_ix, g_ix, sel]
        tok = sel[..., None] * L_SEL + jnp.arange(L_SEL)
        dist_s = q_pos[None, None, :, None, None] - tok
        bias_s = jnp.moveaxis(table_g[t5_bucket(dist_s), g_ix[..., None]], -1, 2).astype(jnp.float32)
        mask_s = (dist_s >= 0)[:, :, None]
        s2 = jnp.einsum('btgrd,bgtnld->bgrtnl', qg, kg).astype(jnp.float32) * SCALE + bias_s
        s2 = jnp.where(mask_s, s2, NEG).reshape(b, G_A, HG_A, tb, n_sel * L_SEL)
        p2 = jax.nn.softmax(s2, axis=-1)
        o_sel = jnp.einsum('bgrtk,bgtkd->btgrd', p2.astype(vg.dtype), vg.reshape(b, G_A, tb, n_sel * L_SEL, DH))
        return (o_cmp.reshape(b, tb, H_A, DH), o_sel.reshape(b, tb, H_A, DH))

    return over_query_blocks(blk, q_pos, q)


def fox_attend(q, dq, q_pos, k, v, dk, k_pos):
    b = q.shape[0]
    dk_t = jnp.moveaxis(dk, 2, 1)[:, :, None, None, :]

    def blk(q_pos, qb, dqb):
        tb = q_pos.shape[0]
        bias = jnp.moveaxis(dqb, 2, 1)[:, :, None, :, None] - dk_t
        mask = k_pos[None, :] <= q_pos[:, None]
        o = masked_attend(qb.reshape(b, tb, H_B, 1, DH), k, v, bias, mask)
        return o.reshape(b, tb, H_B, DH)

    return over_query_blocks(blk, q_pos, q, dq)


def peer(h, w_q, sub_keys, u, v):
    shp = h.shape
    hf = h.reshape(-1, shp[-1])
    n = hf.shape[0]
    hf = jnp.pad(hf, ((0, (-n) % PEER_TOKEN_BLOCK), (0, 0)))

    def blk(hb):
        qh = (hb @ w_q).reshape(-1, PEER_HEADS, 2, D_HALF)
        s = jnp.einsum('nphd,phkd->nphk', qh, sub_keys).astype(jnp.float32)
        s1, i1 = lax.top_k(s[:, :, 0], PEER_TOPK)
        s2, i2 = lax.top_k(s[:, :, 1], PEER_TOPK)
        cand = (s1[..., :, None] + s2[..., None, :]).reshape(-1, PEER_HEADS, PEER_TOPK * PEER_TOPK)
        cidx = (i1[..., :, None] * N_KEYS + i2[..., None, :]).reshape(-1, PEER_HEADS, PEER_TOPK * PEER_TOPK)
        best, j = lax.top_k(cand, PEER_TOPK)
        eidx = jnp.take_along_axis(cidx, j, axis=-1)
        gate = jax.nn.softmax(best, axis=-1)
        act = jax.nn.gelu(jnp.einsum('nd,npkd->npk', hb, u[eidx]).astype(jnp.float32))
        return jnp.einsum('npk,npkd->nd', (gate * act).astype(hb.dtype), v[eidx])

    out = lax.map(blk, hf.reshape(-1, PEER_TOKEN_BLOCK, shp[-1]))
    return out.reshape(-1, shp[-1])[:n].reshape(shp)


def project(h, w_in_l):
    b, t = h.shape[:2]
    (q_a, kv_c, kv_s, kv_w, g_nsa, q_b, k_b, v_b, f_b, gate_a, gate_b) = jnp.split(
        h @ w_in_l, np.cumsum(SPLIT_SIZES)[:-1].tolist(), axis=-1)
    kv = lambda a: a.reshape(b, t, 2, G_A, DH)
    hd = lambda a, nh: a.reshape(b, t, nh, DH)
    return (hd(q_a, H_A), kv(kv_c), kv(kv_s), kv(kv_w), g_nsa.reshape(b, t, 3, H_A),
            hd(q_b, H_B), hd(k_b, H_B), hd(v_b, H_B), f_b, gate_a, gate_b)


def layer_tail(x, o_cmp, o_sel, o_win, g_nsa, o_b, gate_a, gate_b, p_l, lw):
    (w_pa, w_pb, w_o, g_f, wq, sk, u, v, g_p, w_pe, w_pg) = lw
    b, t = x.shape[:2]
    gn = jax.nn.sigmoid(g_nsa)[..., None]
    o_a = gn[:, :, 0] * o_cmp + gn[:, :, 1] * o_sel + gn[:, :, 2] * o_win
    mix = (jax.nn.sigmoid(gate_a) * (o_a.reshape(b, t, W_A) @ w_pa)
           + jax.nn.sigmoid(gate_b) * (o_b.reshape(b, t, W_B) @ w_pb))
    x = x + mix @ w_o
    x = x + peer(rms_norm(x, g_f), wq, sk, u, v)
    return x + (p_l @ w_pe) * jax.nn.sigmoid(rms_norm(x, g_p) @ w_pg)


def setup_inputs(seed: int = 0) -> dict:
    key = jax.random.key(seed)
    ks = iter(jax.random.split(key, 48))
    nrm = lambda shape, scale=1.0: jax.random.normal(next(ks), shape, jnp.float32) * scale
    n_pages = PAST_LEN // PAGE_SIZE
    n_used = DEC_BATCH * n_pages
    n_pool = n_used + n_used // 4
    win_buf = min(WINDOW, PAST_LEN)
    page_table = jax.random.permutation(next(ks), n_pool)[:n_used].reshape(DEC_BATCH, n_pages).astype(jnp.int32)
    return {
        'x_prompt': nrm((BATCH, SEQ, D_MODEL)),
        'x_sample': nrm((DEC_BATCH, DEC_SEQ, D_MODEL)),
        'cache_nsa_cmp_kv': nrm((DEPTH, n_pool, PAGE_SIZE, 2, G_A, DH)),
        'cache_nsa_sel_kv': nrm((DEPTH, n_pool, PAGE_SIZE, 2, G_A, DH)),
        'cache_fox_kv': nrm((DEPTH, n_pool, PAGE_SIZE, 2, H_B, DH)),
        'cache_fox_logf': jax.nn.log_sigmoid(3.0 + nrm((DEPTH, n_pool, PAGE_SIZE, H_B), 0.5)),
        'state_nsa_win_kv': nrm((DEPTH, DEC_BATCH, win_buf, 2, G_A, DH)),
        'page_table': page_table,
        'p_prompt': nrm((DEPTH, BATCH, SEQ, D_PLE)),
        'p_sample': nrm((DEPTH, DEC_BATCH, DEC_SEQ, D_PLE)),
        'g_mix': 1.0 + nrm((DEPTH, D_MODEL), 0.02),
        'w_in': nrm((DEPTH, D_MODEL, N_IN), D_MODEL ** -0.5),
        'b_forget': 3.0 + nrm((DEPTH, H_B), 0.5),
        'w_cmp_k': nrm((DEPTH, L_CMP, DH, DH), (L_CMP * DH) ** -0.5),
        'w_cmp_v': nrm((DEPTH, L_CMP, DH, DH), (L_CMP * DH) ** -0.5),
        'pe_cmp': nrm((DEPTH, L_CMP, DH), 0.1),
        'rel_bias_table': nrm((N_BUCKETS, H_A), 0.5),
        'w_proj_a': nrm((DEPTH, W_A, D_MODEL), W_A ** -0.5),
        'w_proj_b': nrm((DEPTH, W_B, D_MODEL), W_B ** -0.5),
        'w_out': nrm((DEPTH, D_MODEL, D_MODEL), D_MODEL ** -0.5),
        'g_ffn': 1.0 + nrm((DEPTH, D_MODEL), 0.02),
        'peer_w_q': nrm((DEPTH, D_MODEL, PEER_HEADS * D_KEY), D_MODEL ** -0.5),
        'peer_sub_keys': nrm((DEPTH, PEER_HEADS, 2, N_KEYS, D_HALF), D_HALF ** -0.5),
        'peer_u': nrm((DEPTH, N_EXPERTS, D_MODEL), D_MODEL ** -0.5),
        'peer_v': nrm((DEPTH, N_EXPERTS, D_MODEL), PEER_HEADS ** -0.5),
        'g_ple': 1.0 + nrm((DEPTH, D_MODEL), 0.02),
        'w_ple': nrm((DEPTH, D_PLE, D_MODEL), D_PLE ** -0.5),
        'w_ple_gate': nrm((DEPTH, D_MODEL, D_MODEL), D_MODEL ** -0.5),
        'g_final': 1.0 + nrm((D_MODEL,), 0.02),
    }


def reference(x_prompt, x_sample, cache_nsa_cmp_kv, cache_nsa_sel_kv, cache_fox_kv, cache_fox_logf,
              state_nsa_win_kv, page_table, p_prompt, p_sample, g_mix, w_in, b_forget, w_cmp_k, w_cmp_v,
              pe_cmp, rel_bias_table, w_proj_a, w_proj_b, w_out, g_ffn, peer_w_q, peer_sub_keys,
              peer_u, peer_v, g_ple, w_ple, w_ple_gate, g_final):
    f32 = jnp.float32
    tp = x_prompt.shape[1]
    bs, ts = x_sample.shape[:2]
    past = page_table.shape[1] * PAGE_SIZE
    win_p = min(WINDOW, tp)
    win_buf = state_nsa_win_kv.shape[2]
    pos_p = jnp.arange(tp)
    pos_s = past + jnp.arange(ts)
    pos_all = jnp.arange(past + ts)
    pos_w = past - win_buf + jnp.arange(win_buf + ts)
    xp, xs = x_prompt, x_sample
    pc, psel, pw, pfk, pfl = [], [], [], [], []
    sc, ssel, sw, sfk, sfl = [], [], [], [], []
    for l in range(DEPTH):
        lw = (w_proj_a[l], w_proj_b[l], w_out[l], g_ffn[l], peer_w_q[l], peer_sub_keys[l],
              peer_u[l], peer_v[l], g_ple[l], w_ple[l], w_ple_gate[l])
        cmp_w = (w_cmp_k[l], w_cmp_v[l], pe_cmp[l], rel_bias_table)

        (q_a, kv_c, kv_s, kv_w, g_nsa, q_b, k_b, v_b, f_b, gate_a, gate_b) = project(rms_norm(xp, g_mix[l]), w_in[l])
        o_cmp, o_sel = nsa_cmp_sel(q_a, pos_p, kv_c[:, :, 0], kv_c[:, :, 1], kv_s[:, :, 0], kv_s[:, :, 1], *cmp_w)
        o_win = window_prompt(q_a, kv_w[:, :, 0], kv_w[:, :, 1], rel_bias_table)
        logf = jax.nn.log_sigmoid(f_b.astype(f32) + b_forget[l].astype(f32))
        dcum = jnp.cumsum(logf, axis=1)
        o_b = fox_attend(q_b, dcum, pos_p, k_b, v_b, dcum, pos_p)
        xp = layer_tail(xp, o_cmp, o_sel, o_win, g_nsa, o_b, gate_a, gate_b, p_prompt[l], lw)
        pc.append(kv_c)
        psel.append(kv_s)
        pw.append(kv_w[:, tp - win_p:])
        pfk.append(jnp.stack([k_b, v_b], axis=2))
        pfl.append(logf)

        (q_a, kv_c, kv_s, kv_w, g_nsa, q_b, k_b, v_b, f_b, gate_a, gate_b) = project(rms_norm(xs, g_mix[l]), w_in[l])
        kvc_all = jnp.concatenate([cache_nsa_cmp_kv[l, page_table].reshape(bs, past, 2, G_A, DH).astype(kv_c.dtype), kv_c], axis=1)
        kvs_all = jnp.concatenate([cache_nsa_sel_kv[l, page_table].reshape(bs, past, 2, G_A, DH).astype(kv_s.dtype), kv_s], axis=1)
        o_cmp, o_sel = nsa_cmp_sel(q_a, pos_s, kvc_all[:, :, 0], kvc_all[:, :, 1], kvs_all[:, :, 0], kvs_all[:, :, 1], *cmp_w)
        w_all = jnp.concatenate([state_nsa_win_kv[l].astype(kv_w.dtype), kv_w], axis=1)
        o_win = window_attend(q_a, pos_s, w_all[:, :, 0], w_all[:, :, 1], pos_w, rel_bias_table)
        logf = jax.nn.log_sigmoid(f_b.astype(f32) + b_forget[l].astype(f32))
        kvf_past = cache_fox_kv[l, page_table].reshape(bs, past, 2, H_B, DH).astype(k_b.dtype)
        lf_all = jnp.concatenate([cache_fox_logf[l, page_table].reshape(bs, past, H_B).astype(f32), logf], axis=1)
        dk = jnp.cumsum(lf_all, axis=1)
        o_b = fox_attend(q_b, dk[:, past:], pos_s,
                         jnp.concatenate([kvf_past[:, :, 0], k_b], axis=1),
                         jnp.concatenate([kvf_past[:, :, 1], v_b], axis=1), dk, pos_all)
        xs = layer_tail(xs, o_cmp, o_sel, o_win, g_nsa, o_b, gate_a, gate_b, p_sample[l], lw)
        sc.append(kv_c)
        ssel.append(kv_s)
        sw.append(w_all[:, ts:])
        sfk.append(jnp.stack([k_b, v_b], axis=2))
        sfl.append(logf)

    y_prompt = rms_norm(xp, g_final)
    y_sample = rms_norm(xs, g_final)
    return (y_prompt, y_sample,
            jnp.stack(pc), jnp.stack(psel), jnp.stack(pw), jnp.stack(pfk), jnp.stack(pfl),
            jnp.stack(sc), jnp.stack(ssel), jnp.stack(sw), jnp.stack(sfk), jnp.stack(sfl))
```

```python
import functools
import math

import jax
import jax.numpy as jnp
import numpy as np
from jax import lax
from jax.experimental import pallas as pl
from jax.experimental.pallas import tpu as pltpu

D_MODEL = 1024
PAGE_SIZE = 128
H_A = 8
G_A = 2
HG_A = H_A // G_A
DH = 64
L_CMP = 32
L_SEL = 64
N_SEL = 16
WINDOW = 512
H_B = 8
N_BUCKETS = 32
MAX_DISTANCE = 128
PEER_HEADS = 8
N_KEYS = 128
PEER_TOPK = 16
D_KEY = 256
D_HALF = D_KEY // 2
PEER_TOKEN_BLOCK = 128
Q_BLOCK = 128
EPS = 1e-6
NEG = -1e30
FORCE = 1e4
SCALE = DH ** -0.5
W_A = H_A * DH
W_B = H_B * DH
SPLIT_SIZES = (W_A, 2 * G_A * DH, 2 * G_A * DH, 2 * G_A * DH, 3 * H_A,
               W_B, W_B, W_B, H_B, D_MODEL, D_MODEL)
N_IN = sum(SPLIT_SIZES)

LANES = 128
VMEM_LIMIT = 48 * 1024 * 1024


def _norm_proj_body(x_ref, g_ref, w_ref, o_ref):
    x = x_ref[...]
    ms = jnp.mean(x * x, axis=-1, keepdims=True)
    h = x * lax.rsqrt(ms + EPS) * g_ref[...]
    o_ref[...] = jnp.dot(h.astype(jnp.bfloat16), w_ref[...], preferred_element_type=jnp.float32)


def norm_proj(x2d, g, w_bf16, block_n=256):
    n, d = x2d.shape
    m = w_bf16.shape[1]
    return pl.pallas_call(
        _norm_proj_body,
        grid=(n // block_n,),
        in_specs=[pl.BlockSpec((block_n, d), lambda i: (i, 0)),
                  pl.BlockSpec((1, d), lambda i: (0, 0)),
                  pl.BlockSpec((d, m), lambda i: (0, 0))],
        out_specs=pl.BlockSpec((block_n, m), lambda i: (i, 0)),
        out_shape=jax.ShapeDtypeStruct((n, m), jnp.float32),
        compiler_params=pltpu.CompilerParams(dimension_semantics=("parallel",),
                                             vmem_limit_bytes=VMEM_LIMIT),
        name="norm_proj",
    )(x2d, g.reshape(1, d), w_bf16)


def rms_norm(x, g):
    x32 = x.astype(jnp.float32)
    y = x32 * lax.rsqrt(jnp.mean(x32 * x32, axis=-1, keepdims=True) + EPS)
    return (y * g.astype(jnp.float32)).astype(x.dtype)


def t5_bucket(dist):
    n = jnp.maximum(dist, 0)
    max_exact = N_BUCKETS // 2
    nf = jnp.maximum(n, 1).astype(jnp.float32)
    large = max_exact + (jnp.log(nf / max_exact) / math.log(MAX_DISTANCE / max_exact)
                         * (N_BUCKETS - max_exact)).astype(jnp.int32)
    large = jnp.minimum(large, N_BUCKETS - 1)
    return jnp.where(n < max_exact, n, large)


def rel_bias_grid(dist, table):
    b = table[t5_bucket(dist)].astype(jnp.float32)
    return jnp.moveaxis(b.reshape(*dist.shape, G_A, HG_A), (-2, -1), (0, 1))


def over_query_blocks(fn, q_pos, *qs):
    tq = q_pos.shape[0]
    if tq <= Q_BLOCK or tq % Q_BLOCK:
        return fn(q_pos, *qs)
    nb = tq // Q_BLOCK
    split = lambda a: jnp.moveaxis(a.reshape(a.shape[0], nb, Q_BLOCK, *a.shape[2:]), 1, 0)
    out = lax.map(lambda args: fn(*args), (q_pos.reshape(nb, Q_BLOCK),) + tuple(split(a) for a in qs))
    merge = lambda a: jnp.moveaxis(a, 0, 1).reshape(a.shape[1], tq, *a.shape[3:])
    return jax.tree_util.tree_map(merge, out)


def masked_attend(q, k, v, bias, mask):
    s = jnp.einsum('btgrd,bsgd->bgrts', q, k).astype(jnp.float32) * SCALE + bias
    p = jnp.where(mask, jax.nn.softmax(jnp.where(mask, s, NEG), axis=-1), 0.0)
    return jnp.einsum('bgrts,bsgd->btgrd', p.astype(v.dtype), v)


def window_attend(q, q_pos, k, v, k_pos, table):
    b, tq = q.shape[:2]
    dist = q_pos[:, None] - k_pos[None, :]
    mask = (dist >= 0) & (dist < WINDOW) & (k_pos[None, :] >= 0)
    o = masked_attend(q.reshape(b, tq, G_A, HG_A, DH), k, v, rel_bias_grid(dist, table), mask)
    return o.reshape(b, tq, H_A, DH)


def window_prompt(q, kw, vw, table):
    t = q.shape[1]
    padf = lambda a: jnp.pad(a, ((0, 0), (WINDOW, 0), (0, 0), (0, 0)))
    kp, vp = padf(kw), padf(vw)

    def blk(q_pos, qb):
        span = WINDOW + q_pos.shape[0]
        start = q_pos[0]
        kb = lax.dynamic_slice_in_dim(kp, start, span, axis=1)
        vb = lax.dynamic_slice_in_dim(vp, start, span, axis=1)
        k_pos = start - WINDOW + jnp.arange(span)
        return window_attend(qb, q_pos, kb, vb, k_pos, table)

    return over_query_blocks(blk, jnp.arange(t), q)


def nsa_cmp_sel(q, q_pos, kc, vc, ks, vs, w_ck, w_cv, pe, table):
    b, tk = kc.shape[:2]
    pad = (-tk) % L_SEL
    padt = lambda a: jnp.pad(a, ((0, 0), (0, pad), (0, 0), (0, 0)))
    kc, vc, ks, vs = padt(kc), padt(vc), padt(ks), padt(vs)
    tp = tk + pad
    nb_c, nb_s = tp // L_CMP, tp // L_SEL
    n_sel = min(N_SEL, nb_s)
    blocks = lambda a: a.reshape(b, nb_c, L_CMP, G_A, DH) + pe[:, None, :].astype(a.dtype)
    k_cmp = jnp.einsum('bnlgd,lde->bnge', blocks(kc), w_ck)
    v_cmp = jnp.einsum('bnlgd,lde->bnge', blocks(vc), w_cv)
    end_pos = jnp.arange(nb_c) * L_CMP + (L_CMP - 1)
    ks_b = jnp.moveaxis(ks.reshape(b, nb_s, L_SEL, G_A, DH), 3, 1)
    vs_b = jnp.moveaxis(vs.reshape(b, nb_s, L_SEL, G_A, DH), 3, 1)
    b_ix = jnp.arange(b)[:, None, None, None]
    g_ix = jnp.arange(G_A)[None, :, None, None]
    table_g = table.reshape(N_BUCKETS, G_A, HG_A)

    def blk(q_pos, qb):
        tb = q_pos.shape[0]
        qg = qb.reshape(b, tb, G_A, HG_A, DH)
        dist_c = q_pos[:, None] - end_pos[None, :]
        mask_c = dist_c >= 0
        s = jnp.einsum('btgrd,bngd->bgrtn', qg, k_cmp).astype(jnp.float32) * SCALE + rel_bias_grid(dist_c, table)
        p = jnp.where(mask_c, jax.nn.softmax(jnp.where(mask_c, s, NEG), axis=-1), 0.0)
        o_cmp = jnp.einsum('bgrtn,bngd->btgrd', p.astype(v_cmp.dtype), v_cmp)
        imp = p.sum(2).reshape(b, G_A, tb, nb_s, L_SEL // L_CMP).sum(-1)
        cur = (q_pos // L_SEL)[:, None]
        bidx = jnp.arange(nb_s)[None, :]
        imp = jnp.where((bidx == 0) | (bidx == cur), FORCE, imp)
        imp = jnp.where(bidx <= cur, imp, NEG)
        _, sel = lax.top_k(imp, n_sel)
        kg = ks_b[b_ix, g_ix, sel]
        vg = vs_b[b_ix, g_ix, sel]
        tok = sel[..., None] * L_SEL + jnp.arange(L_SEL)
        dist_s = q_pos[None, None, :, None, None] - tok
        bias_s = jnp.moveaxis(table_g[t5_bucket(dist_s), g_ix[..., None]], -1, 2).astype(jnp.float32)
        mask_s = (dist_s >= 0)[:, :, None]
        s2 = jnp.einsum('btgrd,bgtnld->bgrtnl', qg, kg).astype(jnp.float32) * SCALE + bias_s
        s2 = jnp.where(mask_s, s2, NEG).reshape(b, G_A, HG_A, tb, n_sel * L_SEL)
        p2 = jax.nn.softmax(s2, axis=-1)
        o_sel = jnp.einsum('bgrtk,bgtkd->btgrd', p2.astype(vg.dtype), vg.reshape(b, G_A, tb, n_sel * L_SEL, DH))
        return (o_cmp.reshape(b, tb, H_A, DH), o_sel.reshape(b, tb, H_A, DH))

    return over_query_blocks(blk, q_pos, q)


def fox_attend(q, dq, q_pos, k, v, dk, k_pos):
    b = q.shape[0]
    dk_t = jnp.moveaxis(dk, 2, 1)[:, :, None, None, :]

    def blk(q_pos, qb, dqb):
        tb = q_pos.shape[0]
        bias = jnp.moveaxis(dqb, 2, 1)[:, :, None, :, None] - dk_t
        mask = k_pos[None, :] <= q_pos[:, None]
        o = masked_attend(qb.reshape(b, tb, H_B, 1, DH), k, v, bias, mask)
        return o.reshape(b, tb, H_B, DH)

    return over_query_blocks(blk, q_pos, q, dq)


def peer(h, w_q, sub_keys, u, v):
    shp = h.shape
    hf = h.reshape(-1, shp[-1])
    n = hf.shape[0]
    hf = jnp.pad(hf, ((0, (-n) % PEER_TOKEN_BLOCK), (0, 0)))

    def blk(hb):
        qh = (hb @ w_q).reshape(-1, PEER_HEADS, 2, D_HALF)
        s = jnp.einsum('nphd,phkd->nphk', qh, sub_keys).astype(jnp.float32)
        s1, i1 = lax.top_k(s[:, :, 0], PEER_TOPK)
        s2, i2 = lax.top_k(s[:, :, 1], PEER_TOPK)
        cand = (s1[..., :, None] + s2[..., None, :]).reshape(-1, PEER_HEADS, PEER_TOPK * PEER_TOPK)
        cidx = (i1[..., :, None] * N_KEYS + i2[..., None, :]).reshape(-1, PEER_HEADS, PEER_TOPK * PEER_TOPK)
        best, j = lax.top_k(cand, PEER_TOPK)
        eidx = jnp.take_along_axis(cidx, j, axis=-1)
        gate = jax.nn.softmax(best, axis=-1)
        act = jax.nn.gelu(jnp.einsum('nd,npkd->npk', hb, u[eidx]).astype(jnp.float32))
        return jnp.einsum('npk,npkd->nd', (gate * act).astype(hb.dtype), v[eidx])

    out = lax.map(blk, hf.reshape(-1, PEER_TOKEN_BLOCK, shp[-1]))
    return out.reshape(-1, shp[-1])[:n].reshape(shp)


def project(x, g, w_in_pad):
    b, t = x.shape[:2]
    z = norm_proj(x.reshape(b * t, D_MODEL), g, w_in_pad)[:, :N_IN].reshape(b, t, N_IN)
    (q_a, kv_c, kv_s, kv_w, g_nsa, q_b, k_b, v_b, f_b, gate_a, gate_b) = jnp.split(
        z, np.cumsum(SPLIT_SIZES)[:-1].tolist(), axis=-1)
    kv = lambda a: a.reshape(b, t, 2, G_A, DH)
    hd = lambda a, nh: a.reshape(b, t, nh, DH)
    return (hd(q_a, H_A), kv(kv_c), kv(kv_s), kv(kv_w), g_nsa.reshape(b, t, 3, H_A),
            hd(q_b, H_B), hd(k_b, H_B), hd(v_b, H_B), f_b, gate_a, gate_b)


def layer_tail(x, o_cmp, o_sel, o_win, g_nsa, o_b, gate_a, gate_b, p_l, lw):
    (w_pa, w_pb, w_o, g_f, wq, sk, u, v, g_p, w_pe, w_pg) = lw
    b, t = x.shape[:2]
    gn = jax.nn.sigmoid(g_nsa)[..., None]
    o_a = gn[:, :, 0] * o_cmp + gn[:, :, 1] * o_sel + gn[:, :, 2] * o_win
    mix = (jax.nn.sigmoid(gate_a) * (o_a.reshape(b, t, W_A) @ w_pa)
           + jax.nn.sigmoid(gate_b) * (o_b.reshape(b, t, W_B) @ w_pb))
    x = x + mix @ w_o
    x = x + peer(rms_norm(x, g_f), wq, sk, u, v)
    return x + (p_l @ w_pe) * jax.nn.sigmoid(rms_norm(x, g_p) @ w_pg)


def kernel(x_prompt, x_sample, cache_nsa_cmp_kv, cache_nsa_sel_kv, cache_fox_kv, cache_fox_logf,
           state_nsa_win_kv, page_table, p_prompt, p_sample, g_mix, w_in, b_forget, w_cmp_k, w_cmp_v,
           pe_cmp, rel_bias_table, w_proj_a, w_proj_b, w_out, g_ffn, peer_w_q, peer_sub_keys,
           peer_u, peer_v, g_ple, w_ple, w_ple_gate, g_final):
    f32 = jnp.float32
    depth = w_in.shape[0]
    tp = x_prompt.shape[1]
    bs, ts = x_sample.shape[:2]
    past = page_table.shape[1] * PAGE_SIZE
    win_p = min(WINDOW, tp)
    win_buf = state_nsa_win_kv.shape[2]
    pos_p = jnp.arange(tp)
    pos_s = past + jnp.arange(ts)
    pos_all = jnp.arange(past + ts)
    pos_w = past - win_buf + jnp.arange(win_buf + ts)
    n_in_pad = -(-N_IN // LANES) * LANES
    w_in_pad = jnp.pad(w_in, ((0, 0), (0, 0), (0, n_in_pad - N_IN))).astype(jnp.bfloat16)
    xp, xs = x_prompt, x_sample
    pc, psel, pw, pfk, pfl = [], [], [], [], []
    sc, ssel, sw, sfk, sfl = [], [], [], [], []
    for l in range(depth):
        lw = (w_proj_a[l], w_proj_b[l], w_out[l], g_ffn[l], peer_w_q[l], peer_sub_keys[l],
              peer_u[l], peer_v[l], g_ple[l], w_ple[l], w_ple_gate[l])
        cmp_w = (w_cmp_k[l], w_cmp_v[l], pe_cmp[l], rel_bias_table)

        (q_a, kv_c, kv_s, kv_w, g_nsa, q_b, k_b, v_b, f_b, gate_a, gate_b) = project(xp, g_mix[l], w_in_pad[l])
        o_cmp, o_sel = nsa_cmp_sel(q_a, pos_p, kv_c[:, :, 0], kv_c[:, :, 1], kv_s[:, :, 0], kv_s[:, :, 1], *cmp_w)
        o_win = window_prompt(q_a, kv_w[:, :, 0], kv_w[:, :, 1], rel_bias_table)
        logf = jax.nn.log_sigmoid(f_b.astype(f32) + b_forget[l].astype(f32))
        dcum = jnp.cumsum(logf, axis=1)
        o_b = fox_attend(q_b, dcum, pos_p, k_b, v_b, dcum, pos_p)
        xp = layer_tail(xp, o_cmp, o_sel, o_win, g_nsa, o_b, gate_a, gate_b, p_prompt[l], lw)
        pc.append(kv_c)
        psel.append(kv_s)
        pw.append(kv_w[:, tp - win_p:])
        pfk.append(jnp.stack([k_b, v_b], axis=2))
        pfl.append(logf)

        (q_a, kv_c, kv_s, kv_w, g_nsa, q_b, k_b, v_b, f_b, gate_a, gate_b) = project(xs, g_mix[l], w_in_pad[l])
        kvc_all = jnp.concatenate([cache_nsa_cmp_kv[l, page_table].reshape(bs, past, 2, G_A, DH), kv_c], axis=1)
        kvs_all = jnp.concatenate([cache_nsa_sel_kv[l, page_table].reshape(bs, past, 2, G_A, DH), kv_s], axis=1)
        o_cmp, o_sel = nsa_cmp_sel(q_a, pos_s, kvc_all[:, :, 0], kvc_all[:, :, 1], kvs_all[:, :, 0], kvs_all[:, :, 1], *cmp_w)
        w_all = jnp.concatenate([state_nsa_win_kv[l], kv_w], axis=1)
        o_win = window_attend(q_a, pos_s, w_all[:, :, 0], w_all[:, :, 1], pos_w, rel_bias_table)
        logf = jax.nn.log_sigmoid(f_b.astype(f32) + b_forget[l].astype(f32))
        kvf_past = cache_fox_kv[l, page_table].reshape(bs, past, 2, H_B, DH)
        lf_all = jnp.concatenate([cache_fox_logf[l, page_table].reshape(bs, past, H_B).astype(f32), logf], axis=1)
        dk = jnp.cumsum(lf_all, axis=1)
        o_b = fox_attend(q_b, dk[:, past:], pos_s,
                         jnp.concatenate([kvf_past[:, :, 0], k_b], axis=1),
                         jnp.concatenate([kvf_past[:, :, 1], v_b], axis=1), dk, pos_all)
        xs = layer_tail(xs, o_cmp, o_sel, o_win, g_nsa, o_b, gate_a, gate_b, p_sample[l], lw)
        sc.append(kv_c)
        ssel.append(kv_s)
        sw.append(w_all[:, ts:])
        sfk.append(jnp.stack([k_b, v_b], axis=2))
        sfl.append(logf)

    y_prompt = rms_norm(xp, g_final)
    y_sample = rms_norm(xs, g_final)
    return (y_prompt, y_sample,
            jnp.stack(pc), jnp.stack(psel), jnp.stack(pw), jnp.stack(pfk), jnp.stack(pfl),
            jnp.stack(sc), jnp.stack(ssel), jnp.stack(sw), jnp.stack(sfk), jnp.stack(sfl))
```

```python
import functools
import math

import jax
import jax.numpy as jnp
import numpy as np
from jax import lax
from jax.experimental import pallas as pl
from jax.experimental.pallas import tpu as pltpu

D_MODEL = 1024
PAGE_SIZE = 128
H_A = 8
G_A = 2
HG_A = H_A // G_A
DH = 64
L_CMP = 32
L_SEL = 64
N_SEL = 16
WINDOW = 512
H_B = 8
N_BUCKETS = 32
MAX_DISTANCE = 128
PEER_HEADS = 8
N_KEYS = 128
N_EXPERTS = N_KEYS * N_KEYS
PEER_TOPK = 16
D_KEY = 256
D_HALF = D_KEY // 2
PEER_TOKEN_BLOCK = 128
Q_BLOCK = 128
EPS = 1e-6
NEG = -1e30
FORCE = 1e4
SCALE = DH ** -0.5
W_A = H_A * DH
W_B = H_B * DH
SPLIT_SIZES = (W_A, 2 * G_A * DH, 2 * G_A * DH, 2 * G_A * DH, 3 * H_A,
               W_B, W_B, W_B, H_B, D_MODEL, D_MODEL)
N_IN = sum(SPLIT_SIZES)

LANES = 128
VMEM_LIMIT = 48 * 1024 * 1024
CDT = jnp.bfloat16
f32 = jnp.float32


def _nt_dot(a, b):
    return lax.dot_general(a, b, (((1,), (1,)), ((), ())), preferred_element_type=f32)


def _norm_proj_body(x_ref, g_ref, w_ref, o_ref):
    x = x_ref[...]
    ms = jnp.mean(x * x, axis=-1, keepdims=True)
    h = x * lax.rsqrt(ms + EPS) * g_ref[...]
    o_ref[...] = jnp.dot(h.astype(CDT), w_ref[...], preferred_element_type=f32)


def norm_proj(x2d, g, w, block_n=256):
    n, d = x2d.shape
    m = w.shape[1]
    return pl.pallas_call(
        _norm_proj_body,
        grid=(n // block_n,),
        in_specs=[pl.BlockSpec((block_n, d), lambda i: (i, 0)),
                  pl.BlockSpec((1, d), lambda i: (0, 0)),
                  pl.BlockSpec((d, m), lambda i: (0, 0))],
        out_specs=pl.BlockSpec((block_n, m), lambda i: (i, 0)),
        out_shape=jax.ShapeDtypeStruct((n, m), f32),
        compiler_params=pltpu.CompilerParams(dimension_semantics=("parallel",),
                                             vmem_limit_bytes=VMEM_LIMIT),
        name="norm_proj",
    )(x2d, g.reshape(1, d), w)


def _compress_body(a_ref, pe_ref, w_ref, o_ref):
    o_ref[...] = jnp.dot((a_ref[...] + pe_ref[...]).astype(CDT), w_ref[...], preferred_element_type=f32)


def nsa_compress(kv_c, w_ck, w_cv, pe, block_m=128):
    b, t = kv_c.shape[:2]
    nb_c = t // L_CMP
    row = L_CMP * 2 * G_A * DH
    eye = jnp.eye(2, dtype=f32)
    w_big = jnp.einsum('klde,kK,gG->lkgdKGe', jnp.stack([w_ck, w_cv]), eye, eye).reshape(row, 2 * G_A * DH).astype(CDT)
    pe_big = jnp.broadcast_to(pe[:, None, None, :], (L_CMP, 2, G_A, DH)).reshape(1, row)
    a = kv_c.reshape(b * nb_c, row)
    m = a.shape[0]
    out = pl.pallas_call(
        _compress_body,
        grid=(m // block_m,),
        in_specs=[pl.BlockSpec((block_m, row), lambda i: (i, 0)),
                  pl.BlockSpec((1, row), lambda i: (0, 0)),
                  pl.BlockSpec((row, 2 * G_A * DH), lambda i: (0, 0))],
        out_specs=pl.BlockSpec((block_m, 2 * G_A * DH), lambda i: (i, 0)),
        out_shape=jax.ShapeDtypeStruct((m, 2 * G_A * DH), f32),
        compiler_params=pltpu.CompilerParams(dimension_semantics=("parallel",), vmem_limit_bytes=VMEM_LIMIT),
        name="nsa_compress",
    )(a, pe_big, w_big)
    out = out.reshape(b, nb_c, 2, G_A, DH)
    return out[:, :, 0], out[:, :, 1]


def t5_bucket_np(dist):
    n = np.maximum(dist, 0)
    nf = np.maximum(n, 1).astype(np.float32)
    max_exact = N_BUCKETS // 2
    large = max_exact + (np.log(nf / np.float32(max_exact)) / np.float32(math.log(MAX_DISTANCE / max_exact))
                         * np.float32(N_BUCKETS - max_exact)).astype(np.int32)
    large = np.minimum(large, N_BUCKETS - 1)
    return np.where(n < max_exact, n, large).astype(np.int32)


def cmp_perm(nb_c):
    half = nb_c // 2
    npr = np.arange(nb_c)
    return 2 * (npr % half) + npr // half


def nsa_bias_tables(table, t):
    nb_c = t // L_CMP
    nq = t // Q_BLOCK
    tg = table.reshape(N_BUCKETS, G_A, HG_A)
    end = L_CMP * cmp_perm(nb_c) + (L_CMP - 1)
    tpos = np.arange(t).reshape(nq, 1, Q_BLOCK)
    bc = t5_bucket_np(tpos - end[None, :, None])
    bias_c = jnp.transpose(tg[bc], (3, 4, 0, 1, 2))
    i = np.arange(Q_BLOCK)
    d0 = i[None, :] - i[:, None]
    bs = np.stack([t5_bucket_np(d0), t5_bucket_np(d0 + Q_BLOCK),
                   np.full((Q_BLOCK, Q_BLOCK), N_BUCKETS - 1, np.int32)])
    bias_s = jnp.transpose(tg[bs], (3, 0, 4, 1, 2))
    return bias_c.astype(f32), bias_s.astype(f32)


def _softmax_cols(s, mask):
    s = jnp.where(mask, s, NEG)
    m = jnp.max(s, axis=0, keepdims=True)
    e = jnp.where(mask, jnp.exp(s - m), 0.0)
    l = jnp.sum(e, axis=0, keepdims=True)
    return e / jnp.where(l > 0.0, l, 1.0)


def _nsa_body(q_ref, kc_ref, vc_ref, bc_ref, ks_ref, vst_ref, bs_ref, ocmp_ref, osel_ref,
              m_scr, l_scr, acc_scr, *, nb_c):
    qb = Q_BLOCK
    qi = pl.program_id(2)
    nb_s = nb_c // 2
    tpos = qi * qb + lax.broadcasted_iota(jnp.int32, (1, qb), 1)
    npr = lax.broadcasted_iota(jnp.int32, (nb_c, 1), 0)
    n_of = 2 * (npr % nb_s) + npr // nb_s
    mask_c = tpos >= (L_CMP * n_of + (L_CMP - 1))
    psum = jnp.zeros((nb_c, qb), f32)
    for r in range(HG_A):
        s_t = _nt_dot(kc_ref[0, 0], q_ref[0, 0, r]) + bc_ref[0, r, 0]
        p_t = _softmax_cols(s_t, mask_c)
        psum = psum + p_t
        o_t = jnp.dot(vc_ref[0, 0], p_t.astype(CDT), preferred_element_type=f32)
        ocmp_ref[0, :, r * DH:(r + 1) * DH] = o_t.T
    imp = psum[:nb_s] + psum[nb_s:]
    m_io = lax.broadcasted_iota(jnp.int32, (nb_s, qb), 0)
    cur = tpos // L_SEL
    imp = jnp.where((m_io == 0) | (m_io == cur), FORCE, imp)
    imp = jnp.where(m_io <= cur, imp, NEG)
    m_f = m_io.astype(f32)
    sel = jnp.zeros((nb_s, qb), f32)
    for _ in range(N_SEL):
        mx = jnp.max(imp, axis=0, keepdims=True)
        idx = jnp.min(jnp.where(imp == mx, m_f, float(nb_s)), axis=0, keepdims=True)
        hit = m_f == idx
        sel = jnp.where(hit, 1.0, sel)
        imp = jnp.where(hit, -jnp.inf, imp)
    sel_c = sel.astype(CDT)
    m_scr[...] = jnp.full(m_scr.shape, NEG, f32)
    l_scr[...] = jnp.zeros(l_scr.shape, f32)
    acc_scr[...] = jnp.zeros(acc_scr.shape, f32)
    q4 = q_ref[0, 0].reshape(HG_A * qb, DH)
    key_half = lax.broadcasted_iota(jnp.int32, (qb, nb_s), 0) // L_SEL
    m_col = lax.broadcasted_iota(jnp.int32, (qb, nb_s), 1)
    j_io = lax.broadcasted_iota(jnp.int32, (qb, qb), 0)
    i_io = lax.broadcasted_iota(jnp.int32, (qb, qb), 1)

    def step(kb, bias_idx, causal):
        k0 = pl.multiple_of(kb * qb, qb)
        s_t = _nt_dot(ks_ref[0, 0, pl.ds(k0, qb), :], q4)
        expand = jnp.where(m_col == 2 * kb + key_half, 1.0, 0.0).astype(CDT)
        valid = jnp.dot(expand, sel_c, preferred_element_type=f32) > 0.5
        if causal:
            valid = valid & (j_io <= i_io)
        for r in range(HG_A):
            cs = slice(r * qb, (r + 1) * qb)
            sr = s_t[:, cs] + bs_ref[0, bias_idx, r]
            sr = jnp.where(valid, sr, NEG)
            m_old = m_scr[:, cs]
            m_new = jnp.maximum(m_old, jnp.max(sr, axis=0, keepdims=True))
            alpha = jnp.exp(m_old - m_new)
            p = jnp.exp(sr - m_new)
            l_scr[:, cs] = alpha * l_scr[:, cs] + jnp.sum(p, axis=0, keepdims=True)
            m_scr[:, cs] = m_new
            pv = jnp.dot(vst_ref[0, 0, kb], p.astype(CDT), preferred_element_type=f32)
            acc_scr[:, cs] = alpha * acc_scr[:, cs] + pv

    def far(kb, carry):
        step(kb, 2, False)
        return carry

    lax.fori_loop(0, jnp.maximum(qi - 1, 0), far, 0)

    @pl.when(qi > 0)
    def _():
        step(qi - 1, 1, False)

    step(qi, 0, True)
    for r in range(HG_A):
        cs = slice(r * qb, (r + 1) * qb)
        osel_ref[0, :, r * DH:(r + 1) * DH] = (acc_scr[:, cs] / l_scr[:, cs]).T


def nsa_prompt(q_a, k_cmp, v_cmp, kv_s, bias_c, bias_s):
    b, t, _ = q_a.shape
    qb = Q_BLOCK
    nb_c = t // L_CMP
    nq = t // qb
    perm = cmp_perm(nb_c)
    q = (q_a * SCALE).reshape(b, t, G_A, HG_A, DH).transpose(0, 2, 3, 1, 4).astype(CDT)
    kc = k_cmp[:, perm].transpose(0, 2, 1, 3).astype(CDT)
    vct = v_cmp[:, perm].transpose(0, 2, 3, 1).astype(CDT)
    ks = kv_s[:, :, 0].transpose(0, 2, 1, 3).astype(CDT)
    vst = kv_s[:, :, 1].reshape(b, nq, qb, G_A, DH).transpose(0, 3, 1, 4, 2).astype(CDT)
    out = jax.ShapeDtypeStruct((b, t, H_A * DH), f32)
    ospec = pl.BlockSpec((1, qb, HG_A * DH), lambda bi, g, qi: (bi, qi, g))
    return pl.pallas_call(
        functools.partial(_nsa_body, nb_c=nb_c),
        grid=(b, G_A, nq),
        in_specs=[pl.BlockSpec((1, 1, HG_A, qb, DH), lambda bi, g, qi: (bi, g, 0, qi, 0)),
                  pl.BlockSpec((1, 1, nb_c, DH), lambda bi, g, qi: (bi, g, 0, 0)),
                  pl.BlockSpec((1, 1, DH, nb_c), lambda bi, g, qi: (bi, g, 0, 0)),
                  pl.BlockSpec((1, HG_A, 1, nb_c, qb), lambda bi, g, qi: (g, 0, qi, 0, 0)),
                  pl.BlockSpec((1, 1, t, DH), lambda bi, g, qi: (bi, g, 0, 0)),
                  pl.BlockSpec((1, 1, nq, DH, qb), lambda bi, g, qi: (bi, g, 0, 0, 0)),
                  pl.BlockSpec((1, 3, HG_A, qb, qb), lambda bi, g, qi: (g, 0, 0, 0, 0))],
        out_specs=[ospec, ospec],
        out_shape=[out, out],
        scratch_shapes=[pltpu.VMEM((1, HG_A * qb), f32), pltpu.VMEM((1, HG_A * qb), f32),
                        pltpu.VMEM((DH, HG_A * qb), f32)],
        compiler_params=pltpu.CompilerParams(dimension_semantics=("parallel", "parallel", "arbitrary"),
                                             vmem_limit_bytes=VMEM_LIMIT),
        name="nsa_prompt",
    )(q, kc, vct, bias_c, ks, vst, bias_s)


NO_RANK = 99.0
PEER_CHUNK = 1024
PEER_JT = 32


def _extract_topk(work, iota_f, n_rows, top_ref=None):
    rank = jnp.full(work.shape, NO_RANK, f32)
    for k in range(PEER_TOPK):
        m = jnp.max(work, axis=0, keepdims=True)
        idx = jnp.min(jnp.where(work == m, iota_f, float(n_rows)), axis=0, keepdims=True)
        hit = iota_f == idx
        rank = jnp.where(hit, float(k), rank)
        work = jnp.where(hit, -jnp.inf, work)
        if top_ref is not None:
            top_ref[k:k + 1, :] = m
    return rank


def _peer_route_body(x_ref, g_ref, wq_ref, sk_ref, hb_ref, r2_ref, b_ref, c_ref, a_ref,
                     s_scr, top1_scr, top2_scr, cand_scr):
    x = x_ref[...]
    tn = x.shape[0]
    n_cand = PEER_TOPK * PEER_TOPK
    ms = jnp.mean(x * x, axis=-1, keepdims=True)
    hb = (x * lax.rsqrt(ms + EPS) * g_ref[...]).astype(CDT)
    hb_ref[...] = hb
    iota_k = lax.broadcasted_iota(jnp.int32, (N_KEYS, tn), 0).astype(f32)
    iota_c = lax.broadcasted_iota(jnp.int32, (n_cand, tn), 0).astype(f32)
    for p in range(PEER_HEADS):
        ranks = []
        for half in range(2):
            r0 = p * D_KEY + half * D_HALF
            q_t = _nt_dot(wq_ref[r0:r0 + D_HALF, :], hb)
            s_t = jnp.dot(sk_ref[p * 2 + half], q_t.astype(CDT), preferred_element_type=f32)
            s_scr[half] = s_t
            ranks.append(_extract_topk(s_t, iota_k, N_KEYS, top1_scr if half == 0 else top2_scr))
        top1 = top1_scr[...]
        top2 = top2_scr[...]
        for a in range(PEER_TOPK):
            cand_scr[a * PEER_TOPK:(a + 1) * PEER_TOPK, :] = top1[a:a + 1, :] + top2
        rank_c = _extract_topk(cand_scr[...], iota_c, n_cand)
        picked = jnp.where(rank_c < NO_RANK, 1.0, 0.0)
        e1 = jnp.exp(top1 - top1[0:1, :])
        e2 = jnp.exp(top2 - top2[0:1, :])
        z = jnp.zeros((1, tn), f32)
        c_tile = jnp.zeros((N_KEYS, tn), f32)
        for a in range(PEER_TOPK):
            pk = picked[a * PEER_TOPK:(a + 1) * PEER_TOPK, :]
            cnt = jnp.sum(pk, axis=0, keepdims=True)
            z = z + e1[a:a + 1, :] * jnp.sum(pk * e2, axis=0, keepdims=True)
            c_tile = jnp.where(ranks[0] == float(a), cnt, c_tile)
        r2_ref[p] = ranks[1]
        c_ref[p] = c_tile
        a_ref[p] = jnp.exp(s_scr[0] - top1[0:1, :])
        b_ref[p] = jnp.exp(s_scr[1] - top2[0:1, :]) / z


def peer_route(x2d, g, wq_t, sk, block_n=128):
    n, d = x2d.shape
    tile = jax.ShapeDtypeStruct((PEER_HEADS, N_KEYS, n), f32)
    tspec = pl.BlockSpec((PEER_HEADS, N_KEYS, block_n), lambda i: (0, 0, i))
    return pl.pallas_call(
        _peer_route_body,
        grid=(n // block_n,),
        in_specs=[pl.BlockSpec((block_n, d), lambda i: (i, 0)),
                  pl.BlockSpec((1, d), lambda i: (0, 0)),
                  pl.BlockSpec(wq_t.shape, lambda i: (0, 0)),
                  pl.BlockSpec(sk.shape, lambda i: (0, 0, 0))],
        out_specs=[pl.BlockSpec((block_n, d), lambda i: (i, 0)), tspec, tspec, tspec, tspec],
        out_shape=[jax.ShapeDtypeStruct((n, d), CDT), tile, tile, tile, tile],
        scratch_shapes=[pltpu.VMEM((2, N_KEYS, block_n), f32),
                        pltpu.VMEM((PEER_TOPK, block_n), f32),
                        pltpu.VMEM((PEER_TOPK, block_n), f32),
                        pltpu.VMEM((PEER_TOPK * PEER_TOPK, block_n), f32)],
        compiler_params=pltpu.CompilerParams(dimension_semantics=("parallel",), vmem_limit_bytes=VMEM_LIMIT),
        name="peer_route",
    )(x2d, g.reshape(1, d), wq_t, sk)


def _gelu_tanh(x):
    return 0.5 * x * (1.0 + jnp.tanh(math.sqrt(2.0 / math.pi) * (x + 0.044715 * (x * x * x))))


def _peer_dense_body(x_ref, hb_ref, r2_ref, b_ref, c_ref, a_ref, u_ref, vt_ref, o_ref,
                     acc_ref, ga_ref):
    ch = pl.program_id(1)
    n_i = PEER_CHUNK // N_KEYS

    @pl.when(ch == 0)
    def _():
        acc_ref[...] = jnp.zeros_like(acc_ref)

    hb = hb_ref[...]
    for i in range(n_i):
        gi = ch * n_i + i
        act = _nt_dot(u_ref[i * N_KEYS:(i + 1) * N_KEYS, :], hb)
        c_rows = [c_ref[p, pl.ds(gi, 1), :] for p in range(PEER_HEADS)]
        a_rows = [a_ref[p, pl.ds(gi, 1), :] for p in range(PEER_HEADS)]
        for jt in range(N_KEYS // PEER_JT):
            js = slice(jt * PEER_JT, (jt + 1) * PEER_JT)
            g = None
            for p in range(PEER_HEADS):
                term = jnp.where(r2_ref[p, js, :] < c_rows[p], b_ref[p, js, :], 0.0) * a_rows[p]
                g = term if g is None else g + term
            row0 = i * N_KEYS + jt * PEER_JT
            ga_ref[row0:row0 + PEER_JT, :] = (_gelu_tanh(act[js, :]) * g).astype(CDT)
    acc_ref[...] += jnp.dot(vt_ref[...], ga_ref[...], preferred_element_type=f32)

    @pl.when(ch == pl.num_programs(1) - 1)
    def _():
        o_ref[...] = x_ref[...] + acc_ref[...].T


def peer_dense(x2d, hb, r2, b, c, a, u, vt, block_n=256):
    n, d = x2d.shape
    n_ch = N_EXPERTS // PEER_CHUNK
    tspec = pl.BlockSpec((PEER_HEADS, N_KEYS, block_n), lambda i, k: (0, 0, i))
    return pl.pallas_call(
        _peer_dense_body,
        grid=(n // block_n, n_ch),
        in_specs=[pl.BlockSpec((block_n, d), lambda i, k: (i, 0)),
                  pl.BlockSpec((block_n, d), lambda i, k: (i, 0)),
                  tspec, tspec, tspec, tspec,
                  pl.BlockSpec((PEER_CHUNK, d), lambda i, k: (k, 0)),
                  pl.BlockSpec((d, PEER_CHUNK), lambda i, k: (0, k))],
        out_specs=pl.BlockSpec((block_n, d), lambda i, k: (i, 0)),
        out_shape=jax.ShapeDtypeStruct((n, d), f32),
        scratch_shapes=[pltpu.VMEM((d, block_n), f32),
                        pltpu.VMEM((PEER_CHUNK, block_n), CDT)],
        compiler_params=pltpu.CompilerParams(dimension_semantics=("parallel", "arbitrary"),
                                             vmem_limit_bytes=VMEM_LIMIT),
        name="peer_dense",
    )(x2d, hb, r2, b, c, a, u, vt)


def peer_residual(x, g, peer_w):
    wq_t, sk, u_c, vt_c = peer_w
    shp = x.shape
    x2d = x.reshape(-1, shp[-1])
    hb, r2, b, c, a = peer_route(x2d, g, wq_t, sk)
    return peer_dense(x2d, hb, r2, b, c, a, u_c, vt_c).reshape(shp)


def rms_norm(x, g):
    x32 = x.astype(jnp.float32)
    y = x32 * lax.rsqrt(jnp.mean(x32 * x32, axis=-1, keepdims=True) + EPS)
    return (y * g.astype(jnp.float32)).astype(x.dtype)


def t5_bucket(dist):
    n = jnp.maximum(dist, 0)
    max_exact = N_BUCKETS // 2
    nf = jnp.maximum(n, 1).astype(jnp.float32)
    large = max_exact + (jnp.log(nf / max_exact) / math.log(MAX_DISTANCE / max_exact)
                         * (N_BUCKETS - max_exact)).astype(jnp.int32)
    large = jnp.minimum(large, N_BUCKETS - 1)
    return jnp.where(n < max_exact, n, large)


def rel_bias_grid(dist, table):
    b = table[t5_bucket(dist)].astype(jnp.float32)
    return jnp.moveaxis(b.reshape(*dist.shape, G_A, HG_A), (-2, -1), (0, 1))


def over_query_blocks(fn, q_pos, *qs):
    tq = q_pos.shape[0]
    if tq <= Q_BLOCK or tq % Q_BLOCK:
        return fn(q_pos, *qs)
    nb = tq // Q_BLOCK
    split = lambda a: jnp.moveaxis(a.reshape(a.shape[0], nb, Q_BLOCK, *a.shape[2:]), 1, 0)
    out = lax.map(lambda args: fn(*args), (q_pos.reshape(nb, Q_BLOCK),) + tuple(split(a) for a in qs))
    merge = lambda a: jnp.moveaxis(a, 0, 1).reshape(a.shape[1], tq, *a.shape[3:])
    return jax.tree_util.tree_map(merge, out)


def masked_attend(q, k, v, bias, mask):
    s = jnp.einsum('btgrd,bsgd->bgrts', q, k).astype(jnp.float32) * SCALE + bias
    p = jnp.where(mask, jax.nn.softmax(jnp.where(mask, s, NEG), axis=-1), 0.0)
    return jnp.einsum('bgrts,bsgd->btgrd', p.astype(v.dtype), v)


def window_attend(q, q_pos, k, v, k_pos, table):
    b, tq = q.shape[:2]
    dist = q_pos[:, None] - k_pos[None, :]
    mask = (dist >= 0) & (dist < WINDOW) & (k_pos[None, :] >= 0)
    o = masked_attend(q.reshape(b, tq, G_A, HG_A, DH), k, v, rel_bias_grid(dist, table), mask)
    return o.reshape(b, tq, H_A, DH)


def window_prompt(q, kw, vw, table):
    t = q.shape[1]
    padf = lambda a: jnp.pad(a, ((0, 0), (WINDOW, 0), (0, 0), (0, 0)))
    kp, vp = padf(kw), padf(vw)

    def blk(q_pos, qb):
        span = WINDOW + q_pos.shape[0]
        start = q_pos[0]
        kb = lax.dynamic_slice_in_dim(kp, start, span, axis=1)
        vb = lax.dynamic_slice_in_dim(vp, start, span, axis=1)
        k_pos = start - WINDOW + jnp.arange(span)
        return window_attend(qb, q_pos, kb, vb, k_pos, table)

    return over_query_blocks(blk, jnp.arange(t), q)


def nsa_cmp_sel(q, q_pos, kc, vc, ks, vs, w_ck, w_cv, pe, table):
    b, tk = kc.shape[:2]
    pad = (-tk) % L_SEL
    padt = lambda a: jnp.pad(a, ((0, 0), (0, pad), (0, 0), (0, 0)))
    kc, vc, ks, vs = padt(kc), padt(vc), padt(ks), padt(vs)
    tp = tk + pad
    nb_c, nb_s = tp // L_CMP, tp // L_SEL
    n_sel = min(N_SEL, nb_s)
    blocks = lambda a: a.reshape(b, nb_c, L_CMP, G_A, DH) + pe[:, None, :].astype(a.dtype)
    k_cmp = jnp.einsum('bnlgd,lde->bnge', blocks(kc), w_ck)
    v_cmp = jnp.einsum('bnlgd,lde->bnge', blocks(vc), w_cv)
    end_pos = jnp.arange(nb_c) * L_CMP + (L_CMP - 1)
    ks_b = jnp.moveaxis(ks.reshape(b, nb_s, L_SEL, G_A, DH), 3, 1)
    vs_b = jnp.moveaxis(vs.reshape(b, nb_s, L_SEL, G_A, DH), 3, 1)
    b_ix = jnp.arange(b)[:, None, None, None]
    g_ix = jnp.arange(G_A)[None, :, None, None]
    table_g = table.reshape(N_BUCKETS, G_A, HG_A)

    def blk(q_pos, qb):
        tb = q_pos.shape[0]
        qg = qb.reshape(b, tb, G_A, HG_A, DH)
        dist_c = q_pos[:, None] - end_pos[None, :]
        mask_c = dist_c >= 0
        s = jnp.einsum('btgrd,bngd->bgrtn', qg, k_cmp).astype(jnp.float32) * SCALE + rel_bias_grid(dist_c, table)
        p = jnp.where(mask_c, jax.nn.softmax(jnp.where(mask_c, s, NEG), axis=-1), 0.0)
        o_cmp = jnp.einsum('bgrtn,bngd->btgrd', p.astype(v_cmp.dtype), v_cmp)
        imp = p.sum(2).reshape(b, G_A, tb, nb_s, L_SEL // L_CMP).sum(-1)
        cur = (q_pos // L_SEL)[:, None]
        bidx = jnp.arange(nb_s)[None, :]
        imp = jnp.where((bidx == 0) | (bidx == cur), FORCE, imp)
        imp = jnp.where(bidx <= cur, imp, NEG)
        _, sel = lax.top_k(imp, n_sel)
        kg = ks_b[b_ix, g_ix, sel]
        vg = vs_b[b_ix, g_ix, sel]
        tok = sel[..., None] * L_SEL + jnp.arange(L_SEL)
        dist_s = q_pos[None, None, :, None, None] - tok
        bias_s = jnp.moveaxis(table_g[t5_bucket(dist_s), g_ix[..., None]], -1, 2).astype(jnp.float32)
        mask_s = (dist_s >= 0)[:, :, None]
        s2 = jnp.einsum('btgrd,bgtnld->bgrtnl', qg, kg).astype(jnp.float32) * SCALE + bias_s
        s2 = jnp.where(mask_s, s2, NEG).reshape(b, G_A, HG_A, tb, n_sel * L_SEL)
        p2 = jax.nn.softmax(s2, axis=-1)
        o_sel = jnp.einsum('bgrtk,bgtkd->btgrd', p2.astype(vg.dtype), vg.reshape(b, G_A, tb, n_sel * L_SEL, DH))
        return (o_cmp.reshape(b, tb, H_A, DH), o_sel.reshape(b, tb, H_A, DH))

    return over_query_blocks(blk, q_pos, q)


def fox_attend(q, dq, q_pos, k, v, dk, k_pos):
    b = q.shape[0]
    dk_t = jnp.moveaxis(dk, 2, 1)[:, :, None, None, :]

    def blk(q_pos, qb, dqb):
        tb = q_pos.shape[0]
        bias = jnp.moveaxis(dqb, 2, 1)[:, :, None, :, None] - dk_t
        mask = k_pos[None, :] <= q_pos[:, None]
        o = masked_attend(qb.reshape(b, tb, H_B, 1, DH), k, v, bias, mask)
        return o.reshape(b, tb, H_B, DH)

    return over_query_blocks(blk, q_pos, q, dq)


def project(x, g, w_in_pad):
    b, t = x.shape[:2]
    z = norm_proj(x.reshape(b * t, D_MODEL), g, w_in_pad)[:, :N_IN].reshape(b, t, N_IN)
    (q_a, kv_c, kv_s, kv_w, g_nsa, q_b, k_b, v_b, f_b, gate_a, gate_b) = jnp.split(
        z, np.cumsum(SPLIT_SIZES)[:-1].tolist(), axis=-1)
    kv = lambda a: a.reshape(b, t, 2, G_A, DH)
    hd = lambda a, nh: a.reshape(b, t, nh, DH)
    return (hd(q_a, H_A), kv(kv_c), kv(kv_s), kv(kv_w), g_nsa.reshape(b, t, 3, H_A),
            hd(q_b, H_B), hd(k_b, H_B), hd(v_b, H_B), f_b, gate_a, gate_b)


def layer_tail(x, o_cmp, o_sel, o_win, g_nsa, o_b, gate_a, gate_b, p_l, lw, peer_w):
    (w_pa, w_pb, w_o, g_f, g_p, w_pe, w_pg) = lw
    b, t = x.shape[:2]
    gn = jax.nn.sigmoid(g_nsa)[..., None]
    o_a = gn[:, :, 0] * o_cmp + gn[:, :, 1] * o_sel + gn[:, :, 2] * o_win
    mix = (jax.nn.sigmoid(gate_a) * (o_a.reshape(b, t, W_A) @ w_pa)
           + jax.nn.sigmoid(gate_b) * (o_b.reshape(b, t, W_B) @ w_pb))
    x = x + mix @ w_o
    x = peer_residual(x, g_f, peer_w)
    return x + (p_l @ w_pe) * jax.nn.sigmoid(rms_norm(x, g_p) @ w_pg)


def kernel(x_prompt, x_sample, cache_nsa_cmp_kv, cache_nsa_sel_kv, cache_fox_kv, cache_fox_logf,
           state_nsa_win_kv, page_table, p_prompt, p_sample, g_mix, w_in, b_forget, w_cmp_k, w_cmp_v,
           pe_cmp, rel_bias_table, w_proj_a, w_proj_b, w_out, g_ffn, peer_w_q, peer_sub_keys,
           peer_u, peer_v, g_ple, w_ple, w_ple_gate, g_final):
    depth = w_in.shape[0]
    bp, tp = x_prompt.shape[:2]
    bs, ts = x_sample.shape[:2]
    past = page_table.shape[1] * PAGE_SIZE
    win_p = min(WINDOW, tp)
    win_buf = state_nsa_win_kv.shape[2]
    pos_p = jnp.arange(tp)
    pos_s = past + jnp.arange(ts)
    pos_all = jnp.arange(past + ts)
    pos_w = past - win_buf + jnp.arange(win_buf + ts)
    n_in_pad = -(-N_IN // LANES) * LANES
    w_in_pad = jnp.pad(w_in, ((0, 0), (0, 0), (0, n_in_pad - N_IN))).astype(CDT)
    bias_c, bias_s = nsa_bias_tables(rel_bias_table, tp)
    xp, xs = x_prompt, x_sample
    pc, psel, pw, pfk, pfl = [], [], [], [], []
    sc, ssel, sw, sfk, sfl = [], [], [], [], []
    for l in range(depth):
        lw = (w_proj_a[l], w_proj_b[l], w_out[l], g_ffn[l], g_ple[l], w_ple[l], w_ple_gate[l])
        peer_w = (peer_w_q[l].T.astype(CDT),
                  peer_sub_keys[l].reshape(PEER_HEADS * 2, N_KEYS, D_HALF).astype(CDT),
                  peer_u[l].astype(CDT), peer_v[l].T.astype(CDT))
        cmp_w = (w_cmp_k[l], w_cmp_v[l], pe_cmp[l], rel_bias_table)

        (q_a, kv_c, kv_s, kv_w, g_nsa, q_b, k_b, v_b, f_b, gate_a, gate_b) = project(xp, g_mix[l], w_in_pad[l])
        k_cmp, v_cmp = nsa_compress(kv_c, w_cmp_k[l], w_cmp_v[l], pe_cmp[l])
        o_cmp, o_sel = nsa_prompt(q_a.reshape(bp, tp, W_A), k_cmp, v_cmp, kv_s, bias_c, bias_s)
        o_cmp = o_cmp.reshape(bp, tp, H_A, DH)
        o_sel = o_sel.reshape(bp, tp, H_A, DH)
        o_win = window_prompt(q_a, kv_w[:, :, 0], kv_w[:, :, 1], rel_bias_table)
        logf = jax.nn.log_sigmoid(f_b.astype(f32) + b_forget[l].astype(f32))
        dcum = jnp.cumsum(logf, axis=1)
        o_b = fox_attend(q_b, dcum, pos_p, k_b, v_b, dcum, pos_p)
        xp = layer_tail(xp, o_cmp, o_sel, o_win, g_nsa, o_b, gate_a, gate_b, p_prompt[l], lw, peer_w)
        pc.append(kv_c)
        psel.append(kv_s)
        pw.append(kv_w[:, tp - win_p:])
        pfk.append(jnp.stack([k_b, v_b], axis=2))
        pfl.append(logf)

        (q_a, kv_c, kv_s, kv_w, g_nsa, q_b, k_b, v_b, f_b, gate_a, gate_b) = project(xs, g_mix[l], w_in_pad[l])
        kvc_all = jnp.concatenate([cache_nsa_cmp_kv[l, page_table].reshape(bs, past, 2, G_A, DH), kv_c], axis=1)
        kvs_all = jnp.concatenate([cache_nsa_sel_kv[l, page_table].reshape(bs, past, 2, G_A, DH), kv_s], axis=1)
        o_cmp, o_sel = nsa_cmp_sel(q_a, pos_s, kvc_all[:, :, 0], kvc_all[:, :, 1], kvs_all[:, :, 0], kvs_all[:, :, 1], *cmp_w)
        w_all = jnp.concatenate([state_nsa_win_kv[l], kv_w], axis=1)
        o_win = window_attend(q_a, pos_s, w_all[:, :, 0], w_all[:, :, 1], pos_w, rel_bias_table)
        logf = jax.nn.log_sigmoid(f_b.astype(f32) + b_forget[l].astype(f32))
        kvf_past = cache_fox_kv[l, page_table].reshape(bs, past, 2, H_B, DH)
        lf_all = jnp.concatenate([cache_fox_logf[l, page_table].reshape(bs, past, H_B).astype(f32), logf], axis=1)
        dk = jnp.cumsum(lf_all, axis=1)
        o_b = fox_attend(q_b, dk[:, past:], pos_s,
                         jnp.concatenate([kvf_past[:, :, 0], k_b], axis=1),
                         jnp.concatenate([kvf_past[:, :, 1], v_b], axis=1), dk, pos_all)
        xs = layer_tail(xs, o_cmp, o_sel, o_win, g_nsa, o_b, gate_a, gate_b, p_sample[l], lw, peer_w)
        sc.append(kv_c)
        ssel.append(kv_s)
        sw.append(w_all[:, ts:])
        sfk.append(jnp.stack([k_b, v_b], axis=2))
        sfl.append(logf)

    y_prompt = rms_norm(xp, g_final)
    y_sample = rms_norm(xs, g_final)
    return (y_prompt, y_sample,
            jnp.stack(pc), jnp.stack(psel), jnp.stack(pw), jnp.stack(pfk), jnp.stack(pfl),
            jnp.stack(sc), jnp.stack(ssel), jnp.stack(sw), jnp.stack(sfk), jnp.stack(sfl))
```

```python
import functools
import math

import jax
import jax.numpy as jnp
import numpy as np
from jax import lax
from jax.experimental import pallas as pl
from jax.experimental.pallas import tpu as pltpu

D_MODEL = 1024
PAGE_SIZE = 128
H_A = 8
G_A = 2
HG_A = H_A // G_A
DH = 64
L_CMP = 32
L_SEL = 64
N_SEL = 16
WINDOW = 512
H_B = 8
N_BUCKETS = 32
MAX_DISTANCE = 128
PEER_HEADS = 8
N_KEYS = 128
N_EXPERTS = N_KEYS * N_KEYS
PEER_TOPK = 16
D_KEY = 256
D_HALF = D_KEY // 2
PEER_TOKEN_BLOCK = 128
Q_BLOCK = 128
EPS = 1e-6
NEG = -1e30
FORCE = 1e4
SCALE = DH ** -0.5
W_A = H_A * DH
W_B = H_B * DH
SPLIT_SIZES = (W_A, 2 * G_A * DH, 2 * G_A * DH, 2 * G_A * DH, 3 * H_A,
               W_B, W_B, W_B, H_B, D_MODEL, D_MODEL)
N_IN = sum(SPLIT_SIZES)

LANES = 128
VMEM_LIMIT = 48 * 1024 * 1024
CDT = jnp.bfloat16
f32 = jnp.float32


def _nt_dot(a, b):
    return lax.dot_general(a, b, (((1,), (1,)), ((), ())), preferred_element_type=f32)


def _norm_proj_body(x_ref, g_ref, w_ref, o_ref):
    x = x_ref[...]
    ms = jnp.mean(x * x, axis=-1, keepdims=True)
    h = x * lax.rsqrt(ms + EPS) * g_ref[...]
    o_ref[...] = jnp.dot(h.astype(CDT), w_ref[...], preferred_element_type=f32)


def norm_proj(x2d, g, w, block_n=256):
    n, d = x2d.shape
    m = w.shape[1]
    return pl.pallas_call(
        _norm_proj_body,
        grid=(n // block_n,),
        in_specs=[pl.BlockSpec((block_n, d), lambda i: (i, 0)),
                  pl.BlockSpec((1, d), lambda i: (0, 0)),
                  pl.BlockSpec((d, m), lambda i: (0, 0))],
        out_specs=pl.BlockSpec((block_n, m), lambda i: (i, 0)),
        out_shape=jax.ShapeDtypeStruct((n, m), f32),
        compiler_params=pltpu.CompilerParams(dimension_semantics=("parallel",),
                                             vmem_limit_bytes=VMEM_LIMIT),
        name="norm_proj",
    )(x2d, g.reshape(1, d), w)


def _compress_body(a_ref, pe_ref, w_ref, o_ref):
    o_ref[...] = jnp.dot((a_ref[...] + pe_ref[...]).astype(CDT), w_ref[...], preferred_element_type=f32)


def nsa_compress(kv_c, w_ck, w_cv, pe, block_m=128):
    b, t = kv_c.shape[:2]
    nb_c = t // L_CMP
    row = L_CMP * 2 * G_A * DH
    eye = jnp.eye(2, dtype=f32)
    w_big = jnp.einsum('klde,kK,gG->lkgdKGe', jnp.stack([w_ck, w_cv]), eye, eye).reshape(row, 2 * G_A * DH).astype(CDT)
    pe_big = jnp.broadcast_to(pe[:, None, None, :], (L_CMP, 2, G_A, DH)).reshape(1, row)
    a = kv_c.reshape(b * nb_c, row)
    m = a.shape[0]
    out = pl.pallas_call(
        _compress_body,
        grid=(m // block_m,),
        in_specs=[pl.BlockSpec((block_m, row), lambda i: (i, 0)),
                  pl.BlockSpec((1, row), lambda i: (0, 0)),
                  pl.BlockSpec((row, 2 * G_A * DH), lambda i: (0, 0))],
        out_specs=pl.BlockSpec((block_m, 2 * G_A * DH), lambda i: (i, 0)),
        out_shape=jax.ShapeDtypeStruct((m, 2 * G_A * DH), f32),
        compiler_params=pltpu.CompilerParams(dimension_semantics=("parallel",), vmem_limit_bytes=VMEM_LIMIT),
        name="nsa_compress",
    )(a, pe_big, w_big)
    out = out.reshape(b, nb_c, 2, G_A, DH)
    return out[:, :, 0], out[:, :, 1]


def t5_bucket_np(dist):
    n = np.maximum(dist, 0)
    nf = np.maximum(n, 1).astype(np.float32)
    max_exact = N_BUCKETS // 2
    large = max_exact + (np.log(nf / np.float32(max_exact)) / np.float32(math.log(MAX_DISTANCE / max_exact))
                         * np.float32(N_BUCKETS - max_exact)).astype(np.int32)
    large = np.minimum(large, N_BUCKETS - 1)
    return np.where(n < max_exact, n, large).astype(np.int32)


def cmp_perm(nb_c):
    half = nb_c // 2
    npr = np.arange(nb_c)
    return 2 * (npr % half) + npr // half


def nsa_bias_tables(table, t):
    nb_c = t // L_CMP
    nq = t // Q_BLOCK
    tg = table.reshape(N_BUCKETS, G_A, HG_A)
    end = L_CMP * cmp_perm(nb_c) + (L_CMP - 1)
    tpos = np.arange(t).reshape(nq, 1, Q_BLOCK)
    bc = t5_bucket_np(tpos - end[None, :, None])
    bias_c = jnp.transpose(tg[bc], (3, 4, 0, 1, 2))
    i = np.arange(Q_BLOCK)
    d0 = i[None, :] - i[:, None]
    bs = np.stack([t5_bucket_np(d0), t5_bucket_np(d0 + Q_BLOCK),
                   np.full((Q_BLOCK, Q_BLOCK), N_BUCKETS - 1, np.int32)])
    bias_s = jnp.transpose(tg[bs], (3, 0, 4, 1, 2))
    return bias_c.astype(f32), bias_s.astype(f32)


def _softmax_cols(s, mask):
    s = jnp.where(mask, s, NEG)
    m = jnp.max(s, axis=0, keepdims=True)
    e = jnp.where(mask, jnp.exp(s - m), 0.0)
    l = jnp.sum(e, axis=0, keepdims=True)
    return e / jnp.where(l > 0.0, l, 1.0)


def _nsa_body(q_ref, kc_ref, vc_ref, bc_ref, ks_ref, vst_ref, bs_ref, ocmp_ref, osel_ref,
              m_scr, l_scr, acc_scr, *, nb_c):
    qb = Q_BLOCK
    qi = pl.program_id(2)
    nb_s = nb_c // 2
    tpos = qi * qb + lax.broadcasted_iota(jnp.int32, (1, qb), 1)
    npr = lax.broadcasted_iota(jnp.int32, (nb_c, 1), 0)
    n_of = 2 * (npr % nb_s) + npr // nb_s
    mask_c = tpos >= (L_CMP * n_of + (L_CMP - 1))
    psum = jnp.zeros((nb_c, qb), f32)
    for r in range(HG_A):
        s_t = _nt_dot(kc_ref[0, 0], q_ref[0, 0, r]) + bc_ref[0, r, 0]
        p_t = _softmax_cols(s_t, mask_c)
        psum = psum + p_t
        o_t = jnp.dot(vc_ref[0, 0], p_t.astype(CDT), preferred_element_type=f32)
        ocmp_ref[0, :, r * DH:(r + 1) * DH] = o_t.T
    imp = psum[:nb_s] + psum[nb_s:]
    m_io = lax.broadcasted_iota(jnp.int32, (nb_s, qb), 0)
    cur = tpos // L_SEL
    imp = jnp.where((m_io == 0) | (m_io == cur), FORCE, imp)
    imp = jnp.where(m_io <= cur, imp, NEG)
    m_f = m_io.astype(f32)
    sel = jnp.zeros((nb_s, qb), f32)
    for _ in range(N_SEL):
        mx = jnp.max(imp, axis=0, keepdims=True)
        idx = jnp.min(jnp.where(imp == mx, m_f, float(nb_s)), axis=0, keepdims=True)
        hit = m_f == idx
        sel = jnp.where(hit, 1.0, sel)
        imp = jnp.where(hit, -jnp.inf, imp)
    sel_c = sel.astype(CDT)
    m_scr[...] = jnp.full(m_scr.shape, NEG, f32)
    l_scr[...] = jnp.zeros(l_scr.shape, f32)
    acc_scr[...] = jnp.zeros(acc_scr.shape, f32)
    q4 = q_ref[0, 0].reshape(HG_A * qb, DH)
    key_half = lax.broadcasted_iota(jnp.int32, (qb, nb_s), 0) // L_SEL
    m_col = lax.broadcasted_iota(jnp.int32, (qb, nb_s), 1)
    j_io = lax.broadcasted_iota(jnp.int32, (qb, qb), 0)
    i_io = lax.broadcasted_iota(jnp.int32, (qb, qb), 1)

    def step(kb, bias_idx, causal):
        k0 = pl.multiple_of(kb * qb, qb)
        s_t = _nt_dot(ks_ref[0, 0, pl.ds(k0, qb), :], q4)
        expand = jnp.where(m_col == 2 * kb + key_half, 1.0, 0.0).astype(CDT)
        valid = jnp.dot(expand, sel_c, preferred_element_type=f32) > 0.5
        if causal:
            valid = valid & (j_io <= i_io)
        for r in range(HG_A):
            cs = slice(r * qb, (r + 1) * qb)
            sr = s_t[:, cs] + bs_ref[0, bias_idx, r]
            sr = jnp.where(valid, sr, NEG)
            m_old = m_scr[:, cs]
            m_new = jnp.maximum(m_old, jnp.max(sr, axis=0, keepdims=True))
            alpha = jnp.exp(m_old - m_new)
            p = jnp.exp(sr - m_new)
            l_scr[:, cs] = alpha * l_scr[:, cs] + jnp.sum(p, axis=0, keepdims=True)
            m_scr[:, cs] = m_new
            pv = jnp.dot(vst_ref[0, 0, kb], p.astype(CDT), preferred_element_type=f32)
            acc_scr[:, cs] = alpha * acc_scr[:, cs] + pv

    def far(kb, carry):
        step(kb, 2, False)
        return carry

    lax.fori_loop(0, jnp.maximum(qi - 1, 0), far, 0)

    @pl.when(qi > 0)
    def _():
        step(qi - 1, 1, False)

    step(qi, 0, True)
    for r in range(HG_A):
        cs = slice(r * qb, (r + 1) * qb)
        osel_ref[0, :, r * DH:(r + 1) * DH] = (acc_scr[:, cs] / l_scr[:, cs]).T


def nsa_prompt(q_a, k_cmp, v_cmp, kv_s, bias_c, bias_s):
    b, t, _ = q_a.shape
    qb = Q_BLOCK
    nb_c = t // L_CMP
    nq = t // qb
    perm = cmp_perm(nb_c)
    q = (q_a * SCALE).reshape(b, t, G_A, HG_A, DH).transpose(0, 2, 3, 1, 4).astype(CDT)
    kc = k_cmp[:, perm].transpose(0, 2, 1, 3).astype(CDT)
    vct = v_cmp[:, perm].transpose(0, 2, 3, 1).astype(CDT)
    ks = kv_s[:, :, 0].transpose(0, 2, 1, 3).astype(CDT)
    vst = kv_s[:, :, 1].reshape(b, nq, qb, G_A, DH).transpose(0, 3, 1, 4, 2).astype(CDT)
    out = jax.ShapeDtypeStruct((b, t, H_A * DH), f32)
    ospec = pl.BlockSpec((1, qb, HG_A * DH), lambda bi, g, qi: (bi, qi, g))
    return pl.pallas_call(
        functools.partial(_nsa_body, nb_c=nb_c),
        grid=(b, G_A, nq),
        in_specs=[pl.BlockSpec((1, 1, HG_A, qb, DH), lambda bi, g, qi: (bi, g, 0, qi, 0)),
                  pl.BlockSpec((1, 1, nb_c, DH), lambda bi, g, qi: (bi, g, 0, 0)),
                  pl.BlockSpec((1, 1, DH, nb_c), lambda bi, g, qi: (bi, g, 0, 0)),
                  pl.BlockSpec((1, HG_A, 1, nb_c, qb), lambda bi, g, qi: (g, 0, qi, 0, 0)),
                  pl.BlockSpec((1, 1, t, DH), lambda bi, g, qi: (bi, g, 0, 0)),
                  pl.BlockSpec((1, 1, nq, DH, qb), lambda bi, g, qi: (bi, g, 0, 0, 0)),
                  pl.BlockSpec((1, 3, HG_A, qb, qb), lambda bi, g, qi: (g, 0, 0, 0, 0))],
        out_specs=[ospec, ospec],
        out_shape=[out, out],
        scratch_shapes=[pltpu.VMEM((1, HG_A * qb), f32), pltpu.VMEM((1, HG_A * qb), f32),
                        pltpu.VMEM((DH, HG_A * qb), f32)],
        compiler_params=pltpu.CompilerParams(dimension_semantics=("parallel", "parallel", "arbitrary"),
                                             vmem_limit_bytes=VMEM_LIMIT),
        name="nsa_prompt",
    )(q, kc, vct, bias_c, ks, vst, bias_s)


NO_RANK = 99.0
PEER_CHUNK = 1024
PEER_JT = 32


def _extract_topk(work, iota_f, n_rows, top_ref=None):
    rank = jnp.full(work.shape, NO_RANK, f32)
    for k in range(PEER_TOPK):
        m = jnp.max(work, axis=0, keepdims=True)
        idx = jnp.min(jnp.where(work == m, iota_f, float(n_rows)), axis=0, keepdims=True)
        hit = iota_f == idx
        rank = jnp.where(hit, float(k), rank)
        work = jnp.where(hit, -jnp.inf, work)
        if top_ref is not None:
            top_ref[k:k + 1, :] = m
    return rank


def _peer_route_body(x_ref, g_ref, wq_ref, sk_ref, hb_ref, r2_ref, b_ref, c_ref, a_ref,
                     s_scr, top1_scr, top2_scr, cand_scr):
    x = x_ref[...]
    tn = x.shape[0]
    n_cand = PEER_TOPK * PEER_TOPK
    ms = jnp.mean(x * x, axis=-1, keepdims=True)
    hb = (x * lax.rsqrt(ms + EPS) * g_ref[...]).astype(CDT)
    hb_ref[...] = hb
    iota_k = lax.broadcasted_iota(jnp.int32, (N_KEYS, tn), 0).astype(f32)
    iota_c = lax.broadcasted_iota(jnp.int32, (n_cand, tn), 0).astype(f32)
    for p in range(PEER_HEADS):
        ranks = []
        for half in range(2):
            r0 = p * D_KEY + half * D_HALF
            q_t = _nt_dot(wq_ref[r0:r0 + D_HALF, :], hb)
            s_t = jnp.dot(sk_ref[p * 2 + half], q_t.astype(CDT), preferred_element_type=f32)
            s_scr[half] = s_t
            ranks.append(_extract_topk(s_t, iota_k, N_KEYS, top1_scr if half == 0 else top2_scr))
        top1 = top1_scr[...]
        top2 = top2_scr[...]
        for a in range(PEER_TOPK):
            cand_scr[a * PEER_TOPK:(a + 1) * PEER_TOPK, :] = top1[a:a + 1, :] + top2
        rank_c = _extract_topk(cand_scr[...], iota_c, n_cand)
        picked = jnp.where(rank_c < NO_RANK, 1.0, 0.0)
        e1 = jnp.exp(top1 - top1[0:1, :])
        e2 = jnp.exp(top2 - top2[0:1, :])
        z = jnp.zeros((1, tn), f32)
        c_tile = jnp.zeros((N_KEYS, tn), f32)
        for a in range(PEER_TOPK):
            pk = picked[a * PEER_TOPK:(a + 1) * PEER_TOPK, :]
            cnt = jnp.sum(pk, axis=0, keepdims=True)
            z = z + e1[a:a + 1, :] * jnp.sum(pk * e2, axis=0, keepdims=True)
            c_tile = jnp.where(ranks[0] == float(a), cnt, c_tile)
        r2_ref[p] = ranks[1]
        c_ref[p] = c_tile
        a_ref[p] = jnp.exp(s_scr[0] - top1[0:1, :])
        b_ref[p] = jnp.exp(s_scr[1] - top2[0:1, :]) / z


def peer_route(x2d, g, wq_t, sk, block_n=128):
    n, d = x2d.shape
    tile = jax.ShapeDtypeStruct((PEER_HEADS, N_KEYS, n), f32)
    tspec = pl.BlockSpec((PEER_HEADS, N_KEYS, block_n), lambda i: (0, 0, i))
    return pl.pallas_call(
        _peer_route_body,
        grid=(n // block_n,),
        in_specs=[pl.BlockSpec((block_n, d), lambda i: (i, 0)),
                  pl.BlockSpec((1, d), lambda i: (0, 0)),
                  pl.BlockSpec(wq_t.shape, lambda i: (0, 0)),
                  pl.BlockSpec(sk.shape, lambda i: (0, 0, 0))],
        out_specs=[pl.BlockSpec((block_n, d), lambda i: (i, 0)), tspec, tspec, tspec, tspec],
        out_shape=[jax.ShapeDtypeStruct((n, d), CDT), tile, tile, tile, tile],
        scratch_shapes=[pltpu.VMEM((2, N_KEYS, block_n), f32),
                        pltpu.VMEM((PEER_TOPK, block_n), f32),
                        pltpu.VMEM((PEER_TOPK, block_n), f32),
                        pltpu.VMEM((PEER_TOPK * PEER_TOPK, block_n), f32)],
        compiler_params=pltpu.CompilerParams(dimension_semantics=("parallel",), vmem_limit_bytes=VMEM_LIMIT),
        name="peer_route",
    )(x2d, g.reshape(1, d), wq_t, sk)


def _gelu_tanh(x):
    return 0.5 * x * (1.0 + jnp.tanh(math.sqrt(2.0 / math.pi) * (x + 0.044715 * (x * x * x))))


def _peer_dense_body(x_ref, hb_ref, r2_ref, b_ref, c_ref, a_ref, u_ref, vt_ref, o_ref,
                     acc_ref, ga_ref):
    ch = pl.program_id(1)
    n_i = PEER_CHUNK // N_KEYS

    @pl.when(ch == 0)
    def _():
        acc_ref[...] = jnp.zeros_like(acc_ref)

    hb = hb_ref[...]
    for i in range(n_i):
        gi = ch * n_i + i
        act = _nt_dot(u_ref[i * N_KEYS:(i + 1) * N_KEYS, :], hb)
        c_rows = [c_ref[p, pl.ds(gi, 1), :] for p in range(PEER_HEADS)]
        a_rows = [a_ref[p, pl.ds(gi, 1), :] for p in range(PEER_HEADS)]
        for jt in range(N_KEYS // PEER_JT):
            js = slice(jt * PEER_JT, (jt + 1) * PEER_JT)
            g = None
            for p in range(PEER_HEADS):
                term = jnp.where(r2_ref[p, js, :] < c_rows[p], b_ref[p, js, :], 0.0) * a_rows[p]
                g = term if g is None else g + term
            row0 = i * N_KEYS + jt * PEER_JT
            ga_ref[row0:row0 + PEER_JT, :] = (_gelu_tanh(act[js, :]) * g).astype(CDT)
    acc_ref[...] += jnp.dot(vt_ref[...], ga_ref[...], preferred_element_type=f32)

    @pl.when(ch == pl.num_programs(1) - 1)
    def _():
        o_ref[...] = x_ref[...] + acc_ref[...].T


def peer_dense(x2d, hb, r2, b, c, a, u, vt, block_n=256):
    n, d = x2d.shape
    n_ch = N_EXPERTS // PEER_CHUNK
    tspec = pl.BlockSpec((PEER_HEADS, N_KEYS, block_n), lambda i, k: (0, 0, i))
    return pl.pallas_call(
        _peer_dense_body,
        grid=(n // block_n, n_ch),
        in_specs=[pl.BlockSpec((block_n, d), lambda i, k: (i, 0)),
                  pl.BlockSpec((block_n, d), lambda i, k: (i, 0)),
                  tspec, tspec, tspec, tspec,
                  pl.BlockSpec((PEER_CHUNK, d), lambda i, k: (k, 0)),
                  pl.BlockSpec((d, PEER_CHUNK), lambda i, k: (0, k))],
        out_specs=pl.BlockSpec((block_n, d), lambda i, k: (i, 0)),
        out_shape=jax.ShapeDtypeStruct((n, d), f32),
        scratch_shapes=[pltpu.VMEM((d, block_n), f32),
                        pltpu.VMEM((PEER_CHUNK, block_n), CDT)],
        compiler_params=pltpu.CompilerParams(dimension_semantics=("parallel", "arbitrary"),
                                             vmem_limit_bytes=VMEM_LIMIT),
        name="peer_dense",
    )(x2d, hb, r2, b, c, a, u, vt)


def peer_residual(x, g, peer_w):
    wq_t, sk, u_c, vt_c = peer_w
    shp = x.shape
    x2d = x.reshape(-1, shp[-1])
    hb, r2, b, c, a = peer_route(x2d, g, wq_t, sk)
    return peer_dense(x2d, hb, r2, b, c, a, u_c, vt_c).reshape(shp)


FOX_HB = 4


def _fox_body(q_ref, k_ref, vt_ref, dq_ref, dk_ref, o_ref, m_scr, l_scr, acc_scr):
    qb = Q_BLOCK
    qi = pl.program_id(2)
    m_scr[...] = jnp.full(m_scr.shape, NEG, f32)
    l_scr[...] = jnp.zeros(l_scr.shape, f32)
    acc_scr[...] = jnp.zeros(acc_scr.shape, f32)
    j_io = lax.broadcasted_iota(jnp.int32, (qb, qb), 0)
    i_io = lax.broadcasted_iota(jnp.int32, (qb, qb), 1)

    def step(kb, causal):
        k0 = pl.multiple_of(kb * qb, qb)
        for r in range(FOX_HB):
            cs = slice(r * qb, (r + 1) * qb)
            sr = _nt_dot(k_ref[0, r, pl.ds(k0, qb), :], q_ref[0, r])
            sr = sr + (dq_ref[0, r] - dk_ref[0, r, pl.ds(k0, qb), :])
            if causal:
                sr = jnp.where(j_io <= i_io, sr, NEG)
            m_old = m_scr[:, cs]
            m_new = jnp.maximum(m_old, jnp.max(sr, axis=0, keepdims=True))
            alpha = jnp.exp(m_old - m_new)
            p = jnp.exp(sr - m_new)
            l_scr[:, cs] = alpha * l_scr[:, cs] + jnp.sum(p, axis=0, keepdims=True)
            m_scr[:, cs] = m_new
            pv = jnp.dot(vt_ref[0, r, kb], p.astype(CDT), preferred_element_type=f32)
            acc_scr[:, cs] = alpha * acc_scr[:, cs] + pv

    def far(kb, carry):
        step(kb, False)
        return carry

    lax.fori_loop(0, qi, far, 0)
    step(qi, True)
    for r in range(FOX_HB):
        cs = slice(r * qb, (r + 1) * qb)
        o_ref[0, :, r * DH:(r + 1) * DH] = (acc_scr[:, cs] / l_scr[:, cs]).T


def fox_prompt(q_b, k_b, v_b, dcum):
    b, t = q_b.shape[:2]
    qb = Q_BLOCK
    nq = t // qb
    q = (q_b * SCALE).transpose(0, 2, 1, 3).astype(CDT)
    k = k_b.transpose(0, 2, 1, 3).astype(CDT)
    vt = v_b.reshape(b, nq, qb, H_B, DH).transpose(0, 3, 1, 4, 2).astype(CDT)
    d_t = dcum.transpose(0, 2, 1)
    dq = d_t.reshape(b, H_B, nq, 1, qb)
    dk = d_t.reshape(b, H_B, t, 1)
    return pl.pallas_call(
        _fox_body,
        grid=(b, H_B // FOX_HB, nq),
        in_specs=[pl.BlockSpec((1, FOX_HB, qb, DH), lambda bi, g, qi: (bi, g, qi, 0)),
                  pl.BlockSpec((1, FOX_HB, t, DH), lambda bi, g, qi: (bi, g, 0, 0)),
                  pl.BlockSpec((1, FOX_HB, nq, DH, qb), lambda bi, g, qi: (bi, g, 0, 0, 0)),
                  pl.BlockSpec((1, FOX_HB, None, 1, qb), lambda bi, g, qi: (bi, g, qi, 0, 0)),
                  pl.BlockSpec((1, FOX_HB, t, 1), lambda bi, g, qi: (bi, g, 0, 0))],
        out_specs=pl.BlockSpec((1, qb, FOX_HB * DH), lambda bi, g, qi: (bi, qi, g)),
        out_shape=jax.ShapeDtypeStruct((b, t, H_B * DH), f32),
        scratch_shapes=[pltpu.VMEM((1, FOX_HB * qb), f32), pltpu.VMEM((1, FOX_HB * qb), f32),
                        pltpu.VMEM((DH, FOX_HB * qb), f32)],
        compiler_params=pltpu.CompilerParams(dimension_semantics=("parallel", "parallel", "arbitrary"),
                                             vmem_limit_bytes=VMEM_LIMIT),
        name="fox_prompt",
    )(q, k, vt, dq, dk)


ROWS = 64
SEL_COLS = 40


def _online_update(s, v, m_ref, l_ref, acc_ref):
    m_old = m_ref[...]
    m_new = jnp.maximum(m_old, jnp.max(s, axis=-1, keepdims=True))
    alpha = jnp.exp(m_old - m_new)
    p = jnp.exp(s - m_new)
    l_ref[...] = alpha * l_ref[...] + jnp.sum(p, axis=-1, keepdims=True)
    m_ref[...] = m_new
    acc_ref[...] = alpha * acc_ref[...] + jnp.dot(p.astype(CDT), v, preferred_element_type=f32)


def _sample_body(pt_ref, qa_ref, qb_ref, mrow_ref, selc_ref, seln_ref, bsel_ref, foxc_ref, foxn_ref, fb_ref,
                 win_ref, winn_ref, bwin_ref, osel_ref, ofox_ref, owin_ref,
                 ms_ref, ls_ref, as_ref, mf_ref, lf_ref, af_ref, *, past):
    pg = pl.program_id(1)
    n_pg = pl.num_programs(1)
    ps = PAGE_SIZE
    w_a = G_A * DH
    w_b = H_B * DH
    ts = ROWS // H_A
    i_row = lax.broadcasted_iota(jnp.int32, (ROWS, ps), 0) % ts
    j_col = lax.broadcasted_iota(jnp.int32, (ROWS, ps), 1)
    qa = qa_ref[0]
    qb = qb_ref[0]

    @pl.when(pg == 0)
    def _():
        ms_ref[...] = jnp.full(ms_ref.shape, NEG, f32)
        ls_ref[...] = jnp.zeros(ls_ref.shape, f32)
        as_ref[...] = jnp.zeros(as_ref.shape, f32)
        mf_ref[...] = jnp.full(mf_ref.shape, NEG, f32)
        lf_ref[...] = jnp.zeros(lf_ref.shape, f32)
        af_ref[...] = jnp.zeros(af_ref.shape, f32)
        kw = win_ref[0, 0, :, 0:w_a].astype(CDT)
        vw = win_ref[0, 0, :, w_a:2 * w_a].astype(CDT)
        s1 = _nt_dot(qa, kw) + bwin_ref[:, 0:WINDOW]
        iw = lax.broadcasted_iota(jnp.int32, (ROWS, WINDOW), 0) % ts
        jw = lax.broadcasted_iota(jnp.int32, (ROWS, WINDOW), 1)
        s1 = jnp.where(jw > iw, s1, NEG)
        kn = winn_ref[0, :, 0:w_a].astype(CDT)
        vn = winn_ref[0, :, w_a:2 * w_a].astype(CDT)
        s2 = _nt_dot(qa, kn) + bwin_ref[:, WINDOW:WINDOW + ps]
        s2 = jnp.where(j_col <= i_row, s2, NEG)
        m = jnp.maximum(jnp.max(s1, axis=-1, keepdims=True), jnp.max(s2, axis=-1, keepdims=True))
        p1 = jnp.exp(s1 - m)
        p2 = jnp.exp(s2 - m)
        l = jnp.sum(p1, axis=-1, keepdims=True) + jnp.sum(p2, axis=-1, keepdims=True)
        o = (jnp.dot(p1.astype(CDT), vw, preferred_element_type=f32)
             + jnp.dot(p2.astype(CDT), vn, preferred_element_type=f32))
        owin_ref[0] = o / l

    causal = (pg * ps + j_col) <= (past + i_row)

    def sel_step(blk_ref):
        k = blk_ref[0, :, 0:w_a].astype(CDT)
        v = blk_ref[0, :, w_a:2 * w_a].astype(CDT)
        s = _nt_dot(qa, k) + bsel_ref[0]
        blk_of_key = 2 * pg + lax.broadcasted_iota(jnp.int32, (SEL_COLS, ps), 1) // L_SEL
        expand = jnp.where(lax.broadcasted_iota(jnp.int32, (SEL_COLS, ps), 0) == blk_of_key, 1.0, 0.0).astype(CDT)
        picked = jnp.dot(mrow_ref[0].astype(CDT), expand, preferred_element_type=f32) > 0.5
        s = jnp.where(picked & causal, s, NEG)
        _online_update(s, v, ms_ref, ls_ref, as_ref)

    def fox_step(blk_ref):
        k = blk_ref[0, :, 0:w_b].astype(CDT)
        v = blk_ref[0, :, w_b:2 * w_b].astype(CDT)
        s = _nt_dot(qb, k) + fb_ref[0, 0]
        s = jnp.where(causal, s, NEG)
        _online_update(s, v, mf_ref, lf_ref, af_ref)

    @pl.when(pg < n_pg - 1)
    def _():
        sel_step(selc_ref)
        fox_step(foxc_ref)

    @pl.when(pg == n_pg - 1)
    def _():
        sel_step(seln_ref)
        fox_step(foxn_ref)
        osel_ref[0] = as_ref[...] / ls_ref[...]
        ofox_ref[0] = af_ref[...] / lf_ref[...]


def sample_attention(pt_l, qa_bd, qb_bd, mrows, sel_cache2, sel_new, bias_sel, fox_cache2, fox_new, fbias,
                     win_state, win_new, bias_win, *, layer, past):
    bs, n_pages = pt_l.shape
    ps = PAGE_SIZE
    n_pg = n_pages + 1
    w_a = G_A * DH
    w_b = H_B * DH
    last = n_pages - 1
    grid_spec = pltpu.PrefetchScalarGridSpec(
        num_scalar_prefetch=1,
        grid=(bs, n_pg),
        in_specs=[
            pl.BlockSpec((1, ROWS, w_a), lambda b, p, pt: (b, 0, 0)),
            pl.BlockSpec((1, ROWS, w_b), lambda b, p, pt: (b, 0, 0)),
            pl.BlockSpec((1, ROWS, SEL_COLS), lambda b, p, pt: (b, 0, 0)),
            pl.BlockSpec((1, ps, 2 * w_a), lambda b, p, pt: (pt[b, jnp.minimum(p, last)], 0, 0)),
            pl.BlockSpec((1, ps, 2 * w_a), lambda b, p, pt: (b, 0, 0)),
            pl.BlockSpec((1, ROWS, ps), lambda b, p, pt: (p, 0, 0)),
            pl.BlockSpec((1, ps, 2 * w_b), lambda b, p, pt: (pt[b, jnp.minimum(p, last)], 0, 0)),
            pl.BlockSpec((1, ps, 2 * w_b), lambda b, p, pt: (b, 0, 0)),
            pl.BlockSpec((1, 1, ROWS, ps), lambda b, p, pt: (b, p, 0, 0)),
            pl.BlockSpec((1, 1, WINDOW, 2 * w_a), lambda b, p, pt: (layer, b, 0, 0)),
            pl.BlockSpec((1, ps, 2 * w_a), lambda b, p, pt: (b, 0, 0)),
            pl.BlockSpec((ROWS, WINDOW + ps), lambda b, p, pt: (0, 0)),
        ],
        out_specs=[pl.BlockSpec((1, ROWS, w_a), lambda b, p, pt: (b, 0, 0)),
                   pl.BlockSpec((1, ROWS, w_b), lambda b, p, pt: (b, 0, 0)),
                   pl.BlockSpec((1, ROWS, w_a), lambda b, p, pt: (b, 0, 0))],
        scratch_shapes=[pltpu.VMEM((ROWS, 1), f32), pltpu.VMEM((ROWS, 1), f32), pltpu.VMEM((ROWS, w_a), f32),
                        pltpu.VMEM((ROWS, 1), f32), pltpu.VMEM((ROWS, 1), f32), pltpu.VMEM((ROWS, w_b), f32)],
    )
    return pl.pallas_call(
        functools.partial(_sample_body, past=past),
        grid_spec=grid_spec,
        out_shape=[jax.ShapeDtypeStruct((bs, ROWS, w_a), f32), jax.ShapeDtypeStruct((bs, ROWS, w_b), f32),
                   jax.ShapeDtypeStruct((bs, ROWS, w_a), f32)],
        compiler_params=pltpu.CompilerParams(dimension_semantics=("parallel", "arbitrary"),
                                             vmem_limit_bytes=VMEM_LIMIT),
        name="sample_attention",
    )(pt_l, qa_bd, qb_bd, mrows, sel_cache2, sel_new, bias_sel, fox_cache2, fox_new, fbias, win_state, win_new,
      bias_win)


def sample_bias_tables(table, past, ts, n_pg):
    tg = table.reshape(N_BUCKETS, H_A)
    tq = past + np.arange(ts)
    s_abs = np.arange(n_pg * PAGE_SIZE)
    bsel = t5_bucket_np(tq[:, None] - s_abs[None, :])
    k_pos_w = np.concatenate([past - WINDOW + np.arange(WINDOW), past + np.arange(PAGE_SIZE)])
    bwin = t5_bucket_np(tq[:, None] - k_pos_w[None, :])

    def rows(bk):
        return jnp.transpose(tg[bk], (2, 0, 1)).reshape(H_A * ts, bk.shape[1])

    bias_sel = rows(bsel).reshape(ROWS, n_pg, PAGE_SIZE).transpose(1, 0, 2)
    return bias_sel.astype(f32), rows(bwin).astype(f32)


def block_diag_rows(q, heads_per_group):
    bs, ts, h, dh = q.shape
    n_c = h // heads_per_group
    qh = q.transpose(0, 2, 1, 3)
    onehot = (np.arange(h)[:, None] // heads_per_group == np.arange(n_c)[None, :]).astype(np.float32)
    out = qh[:, :, :, None, :] * onehot[None, :, None, :, None]
    return out.reshape(bs, h * ts, n_c * dh)


def diag_extract(o, heads_per_group, ts):
    bs = o.shape[0]
    h = o.shape[1] // ts
    n_c = h // heads_per_group
    o5 = o.reshape(bs, h, ts, n_c, DH)
    own = o5[:, np.arange(h), :, np.arange(h) // heads_per_group, :]
    return own.transpose(1, 2, 0, 3)


def sample_cmp_select(q_a, q_pos, kc, vc, w_ck, w_cv, pe, table):
    b, tk = kc.shape[:2]
    pad = (-tk) % L_SEL
    padt = lambda a: jnp.pad(a, ((0, 0), (0, pad), (0, 0), (0, 0)))
    kc, vc = padt(kc), padt(vc)
    tp = tk + pad
    nb_c, nb_s = tp // L_CMP, tp // L_SEL
    n_sel = min(N_SEL, nb_s)
    blocks = lambda a: a.reshape(b, nb_c, L_CMP, G_A, DH) + pe[:, None, :].astype(a.dtype)
    k_cmp = jnp.einsum('bnlgd,lde->bnge', blocks(kc), w_ck)
    v_cmp = jnp.einsum('bnlgd,lde->bnge', blocks(vc), w_cv)
    end_pos = jnp.arange(nb_c) * L_CMP + (L_CMP - 1)
    tb = q_pos.shape[0]
    qg = q_a.reshape(b, tb, G_A, HG_A, DH)
    dist_c = q_pos[:, None] - end_pos[None, :]
    mask_c = dist_c >= 0
    s = jnp.einsum('btgrd,bngd->bgrtn', qg, k_cmp).astype(f32) * SCALE + rel_bias_grid(dist_c, table)
    p = jnp.where(mask_c, jax.nn.softmax(jnp.where(mask_c, s, NEG), axis=-1), 0.0)
    o_cmp = jnp.einsum('bgrtn,bngd->btgrd', p.astype(v_cmp.dtype), v_cmp)
    imp = p.sum(2).reshape(b, G_A, tb, nb_s, L_SEL // L_CMP).sum(-1)
    cur = (q_pos // L_SEL)[:, None]
    bidx = jnp.arange(nb_s)[None, :]
    imp = jnp.where((bidx == 0) | (bidx == cur), FORCE, imp)
    imp = jnp.where(bidx <= cur, imp, NEG)
    _, sel = lax.top_k(imp, n_sel)
    return o_cmp.reshape(b, tb, H_A, DH), sel


def sample_group_attention(l, q_a, kv_s, kv_w, q_b, k_b, v_b, logf, sel, cache_sel, cache_fox, cache_logf,
                           win_state, page_table, table):
    bs, ts = q_a.shape[:2]
    n_pages = page_table.shape[1]
    past = n_pages * PAGE_SIZE
    n_pool = cache_sel.shape[1]
    ps = PAGE_SIZE
    pt_l = (page_table + l * n_pool).astype(jnp.int32)
    qa_bd = block_diag_rows(q_a * SCALE, HG_A).astype(CDT)
    qb_bd = block_diag_rows(q_b * SCALE, 1).astype(CDT)
    onehot = jnp.sum(jax.nn.one_hot(sel, SEL_COLS, dtype=f32), axis=-2)
    mrows = jnp.broadcast_to(onehot[:, :, None], (bs, G_A, HG_A, ts, SEL_COLS)).reshape(bs, ROWS, SEL_COLS)
    padn = lambda a: jnp.pad(a.reshape(bs, ts, -1), ((0, 0), (0, ps - ts), (0, 0)))
    sel_new = padn(kv_s)
    win_new = padn(kv_w)
    fox_new = padn(jnp.stack([k_b, v_b], axis=2))
    bias_sel, bias_win = sample_bias_tables(table, past, ts, n_pages + 1)
    lf_past = cache_logf[l][page_table].reshape(bs, past, H_B)
    dk = jnp.cumsum(jnp.concatenate([lf_past, logf], axis=1), axis=1)
    dk_pad = jnp.pad(dk, ((0, 0), (0, ps - ts), (0, 0)))
    dq = dk[:, past:]
    fbias = dq.transpose(0, 2, 1)[:, :, :, None] - dk_pad.transpose(0, 2, 1)[:, :, None, :]
    fbias = fbias.reshape(bs, ROWS, n_pages + 1, ps).transpose(0, 2, 1, 3)
    sel_cache2 = cache_sel.reshape(-1, ps, 2 * G_A * DH)
    fox_cache2 = cache_fox.reshape(-1, ps, 2 * H_B * DH)
    win2 = win_state.reshape(win_state.shape[0], bs, WINDOW, 2 * G_A * DH)
    o_sel, o_fox, o_win = sample_attention(pt_l, qa_bd, qb_bd, mrows, sel_cache2, sel_new, bias_sel, fox_cache2,
                                           fox_new, fbias, win2, win_new, bias_win, layer=l, past=past)
    return diag_extract(o_sel, HG_A, ts), diag_extract(o_win, HG_A, ts), diag_extract(o_fox, 1, ts)


def rms_norm(x, g):
    x32 = x.astype(jnp.float32)
    y = x32 * lax.rsqrt(jnp.mean(x32 * x32, axis=-1, keepdims=True) + EPS)
    return (y * g.astype(jnp.float32)).astype(x.dtype)


def t5_bucket(dist):
    n = jnp.maximum(dist, 0)
    max_exact = N_BUCKETS // 2
    nf = jnp.maximum(n, 1).astype(jnp.float32)
    large = max_exact + (jnp.log(nf / max_exact) / math.log(MAX_DISTANCE / max_exact)
                         * (N_BUCKETS - max_exact)).astype(jnp.int32)
    large = jnp.minimum(large, N_BUCKETS - 1)
    return jnp.where(n < max_exact, n, large)


def rel_bias_grid(dist, table):
    b = table[t5_bucket(dist)].astype(jnp.float32)
    return jnp.moveaxis(b.reshape(*dist.shape, G_A, HG_A), (-2, -1), (0, 1))


def over_query_blocks(fn, q_pos, *qs):
    tq = q_pos.shape[0]
    if tq <= Q_BLOCK or tq % Q_BLOCK:
        return fn(q_pos, *qs)
    nb = tq // Q_BLOCK
    split = lambda a: jnp.moveaxis(a.reshape(a.shape[0], nb, Q_BLOCK, *a.shape[2:]), 1, 0)
    out = lax.map(lambda args: fn(*args), (q_pos.reshape(nb, Q_BLOCK),) + tuple(split(a) for a in qs))
    merge = lambda a: jnp.moveaxis(a, 0, 1).reshape(a.shape[1], tq, *a.shape[3:])
    return jax.tree_util.tree_map(merge, out)


def masked_attend(q, k, v, bias, mask):
    s = jnp.einsum('btgrd,bsgd->bgrts', q, k).astype(jnp.float32) * SCALE + bias
    p = jnp.where(mask, jax.nn.softmax(jnp.where(mask, s, NEG), axis=-1), 0.0)
    return jnp.einsum('bgrts,bsgd->btgrd', p.astype(v.dtype), v)


def window_attend(q, q_pos, k, v, k_pos, table):
    b, tq = q.shape[:2]
    dist = q_pos[:, None] - k_pos[None, :]
    mask = (dist >= 0) & (dist < WINDOW) & (k_pos[None, :] >= 0)
    o = masked_attend(q.reshape(b, tq, G_A, HG_A, DH), k, v, rel_bias_grid(dist, table), mask)
    return o.reshape(b, tq, H_A, DH)


def window_prompt(q, kw, vw, table):
    t = q.shape[1]
    padf = lambda a: jnp.pad(a, ((0, 0), (WINDOW, 0), (0, 0), (0, 0)))
    kp, vp = padf(kw), padf(vw)

    def blk(q_pos, qb):
        span = WINDOW + q_pos.shape[0]
        start = q_pos[0]
        kb = lax.dynamic_slice_in_dim(kp, start, span, axis=1)
        vb = lax.dynamic_slice_in_dim(vp, start, span, axis=1)
        k_pos = start - WINDOW + jnp.arange(span)
        return window_attend(qb, q_pos, kb, vb, k_pos, table)

    return over_query_blocks(blk, jnp.arange(t), q)


def nsa_cmp_sel(q, q_pos, kc, vc, ks, vs, w_ck, w_cv, pe, table):
    b, tk = kc.shape[:2]
    pad = (-tk) % L_SEL
    padt = lambda a: jnp.pad(a, ((0, 0), (0, pad), (0, 0), (0, 0)))
    kc, vc, ks, vs = padt(kc), padt(vc), padt(ks), padt(vs)
    tp = tk + pad
    nb_c, nb_s = tp // L_CMP, tp // L_SEL
    n_sel = min(N_SEL, nb_s)
    blocks = lambda a: a.reshape(b, nb_c, L_CMP, G_A, DH) + pe[:, None, :].astype(a.dtype)
    k_cmp = jnp.einsum('bnlgd,lde->bnge', blocks(kc), w_ck)
    v_cmp = jnp.einsum('bnlgd,lde->bnge', blocks(vc), w_cv)
    end_pos = jnp.arange(nb_c) * L_CMP + (L_CMP - 1)
    ks_b = jnp.moveaxis(ks.reshape(b, nb_s, L_SEL, G_A, DH), 3, 1)
    vs_b = jnp.moveaxis(vs.reshape(b, nb_s, L_SEL, G_A, DH), 3, 1)
    b_ix = jnp.arange(b)[:, None, None, None]
    g_ix = jnp.arange(G_A)[None, :, None, None]
    table_g = table.reshape(N_BUCKETS, G_A, HG_A)

    def blk(q_pos, qb):
        tb = q_pos.shape[0]
        qg = qb.reshape(b, tb, G_A, HG_A, DH)
        dist_c = q_pos[:, None] - end_pos[None, :]
        mask_c = dist_c >= 0
        s = jnp.einsum('btgrd,bngd->bgrtn', qg, k_cmp).astype(jnp.float32) * SCALE + rel_bias_grid(dist_c, table)
        p = jnp.where(mask_c, jax.nn.softmax(jnp.where(mask_c, s, NEG), axis=-1), 0.0)
        o_cmp = jnp.einsum('bgrtn,bngd->btgrd', p.astype(v_cmp.dtype), v_cmp)
        imp = p.sum(2).reshape(b, G_A, tb, nb_s, L_SEL // L_CMP).sum(-1)
        cur = (q_pos // L_SEL)[:, None]
        bidx = jnp.arange(nb_s)[None, :]
        imp = jnp.where((bidx == 0) | (bidx == cur), FORCE, imp)
        imp = jnp.where(bidx <= cur, imp, NEG)
        _, sel = lax.top_k(imp, n_sel)
        kg = ks_b[b_ix, g_ix, sel]
        vg = vs_b[b_ix, g_ix, sel]
        tok = sel[..., None] * L_SEL + jnp.arange(L_SEL)
        dist_s = q_pos[None, None, :, None, None] - tok
        bias_s = jnp.moveaxis(table_g[t5_bucket(dist_s), g_ix[..., None]], -1, 2).astype(jnp.float32)
        mask_s = (dist_s >= 0)[:, :, None]
        s2 = jnp.einsum('btgrd,bgtnld->bgrtnl', qg, kg).astype(jnp.float32) * SCALE + bias_s
        s2 = jnp.where(mask_s, s2, NEG).reshape(b, G_A, HG_A, tb, n_sel * L_SEL)
        p2 = jax.nn.softmax(s2, axis=-1)
        o_sel = jnp.einsum('bgrtk,bgtkd->btgrd', p2.astype(vg.dtype), vg.reshape(b, G_A, tb, n_sel * L_SEL, DH))
        return (o_cmp.reshape(b, tb, H_A, DH), o_sel.reshape(b, tb, H_A, DH))

    return over_query_blocks(blk, q_pos, q)


def fox_attend(q, dq, q_pos, k, v, dk, k_pos):
    b = q.shape[0]
    dk_t = jnp.moveaxis(dk, 2, 1)[:, :, None, None, :]

    def blk(q_pos, qb, dqb):
        tb = q_pos.shape[0]
        bias = jnp.moveaxis(dqb, 2, 1)[:, :, None, :, None] - dk_t
        mask = k_pos[None, :] <= q_pos[:, None]
        o = masked_attend(qb.reshape(b, tb, H_B, 1, DH), k, v, bias, mask)
        return o.reshape(b, tb, H_B, DH)

    return over_query_blocks(blk, q_pos, q, dq)


def project(x, g, w_in_pad):
    b, t = x.shape[:2]
    z = norm_proj(x.reshape(b * t, D_MODEL), g, w_in_pad)[:, :N_IN].reshape(b, t, N_IN)
    (q_a, kv_c, kv_s, kv_w, g_nsa, q_b, k_b, v_b, f_b, gate_a, gate_b) = jnp.split(
        z, np.cumsum(SPLIT_SIZES)[:-1].tolist(), axis=-1)
    kv = lambda a: a.reshape(b, t, 2, G_A, DH)
    hd = lambda a, nh: a.reshape(b, t, nh, DH)
    return (hd(q_a, H_A), kv(kv_c), kv(kv_s), kv(kv_w), g_nsa.reshape(b, t, 3, H_A),
            hd(q_b, H_B), hd(k_b, H_B), hd(v_b, H_B), f_b, gate_a, gate_b)


def layer_tail(x, o_cmp, o_sel, o_win, g_nsa, o_b, gate_a, gate_b, p_l, lw, peer_w):
    (w_pa, w_pb, w_o, g_f, g_p, w_pe, w_pg) = lw
    b, t = x.shape[:2]
    gn = jax.nn.sigmoid(g_nsa)[..., None]
    o_a = gn[:, :, 0] * o_cmp + gn[:, :, 1] * o_sel + gn[:, :, 2] * o_win
    mix = (jax.nn.sigmoid(gate_a) * (o_a.reshape(b, t, W_A) @ w_pa)
           + jax.nn.sigmoid(gate_b) * (o_b.reshape(b, t, W_B) @ w_pb))
    x = x + mix @ w_o
    x = peer_residual(x, g_f, peer_w)
    return x + (p_l @ w_pe) * jax.nn.sigmoid(rms_norm(x, g_p) @ w_pg)


def kernel(x_prompt, x_sample, cache_nsa_cmp_kv, cache_nsa_sel_kv, cache_fox_kv, cache_fox_logf,
           state_nsa_win_kv, page_table, p_prompt, p_sample, g_mix, w_in, b_forget, w_cmp_k, w_cmp_v,
           pe_cmp, rel_bias_table, w_proj_a, w_proj_b, w_out, g_ffn, peer_w_q, peer_sub_keys,
           peer_u, peer_v, g_ple, w_ple, w_ple_gate, g_final):
    depth = w_in.shape[0]
    bp, tp = x_prompt.shape[:2]
    bs, ts = x_sample.shape[:2]
    past = page_table.shape[1] * PAGE_SIZE
    win_p = min(WINDOW, tp)
    win_buf = state_nsa_win_kv.shape[2]
    pos_p = jnp.arange(tp)
    pos_s = past + jnp.arange(ts)
    pos_all = jnp.arange(past + ts)
    pos_w = past - win_buf + jnp.arange(win_buf + ts)
    n_in_pad = -(-N_IN // LANES) * LANES
    w_in_pad = jnp.pad(w_in, ((0, 0), (0, 0), (0, n_in_pad - N_IN))).astype(CDT)
    bias_c, bias_s = nsa_bias_tables(rel_bias_table, tp)
    xp, xs = x_prompt, x_sample
    pc, psel, pw, pfk, pfl = [], [], [], [], []
    sc, ssel, sw, sfk, sfl = [], [], [], [], []
    for l in range(depth):
        lw = (w_proj_a[l], w_proj_b[l], w_out[l], g_ffn[l], g_ple[l], w_ple[l], w_ple_gate[l])
        peer_w = (peer_w_q[l].T.astype(CDT),
                  peer_sub_keys[l].reshape(PEER_HEADS * 2, N_KEYS, D_HALF).astype(CDT),
                  peer_u[l].astype(CDT), peer_v[l].T.astype(CDT))
        cmp_w = (w_cmp_k[l], w_cmp_v[l], pe_cmp[l], rel_bias_table)

        (q_a, kv_c, kv_s, kv_w, g_nsa, q_b, k_b, v_b, f_b, gate_a, gate_b) = project(xp, g_mix[l], w_in_pad[l])
        k_cmp, v_cmp = nsa_compress(kv_c, w_cmp_k[l], w_cmp_v[l], pe_cmp[l])
        o_cmp, o_sel = nsa_prompt(q_a.reshape(bp, tp, W_A), k_cmp, v_cmp, kv_s, bias_c, bias_s)
        o_cmp = o_cmp.reshape(bp, tp, H_A, DH)
        o_sel = o_sel.reshape(bp, tp, H_A, DH)
        o_win = window_prompt(q_a, kv_w[:, :, 0], kv_w[:, :, 1], rel_bias_table)
        logf = jax.nn.log_sigmoid(f_b.astype(f32) + b_forget[l].astype(f32))
        dcum = jnp.cumsum(logf, axis=1)
        o_b = fox_prompt(q_b, k_b, v_b, dcum).reshape(bp, tp, H_B, DH)
        xp = layer_tail(xp, o_cmp, o_sel, o_win, g_nsa, o_b, gate_a, gate_b, p_prompt[l], lw, peer_w)
        pc.append(kv_c)
        psel.append(kv_s)
        pw.append(kv_w[:, tp - win_p:])
        pfk.append(jnp.stack([k_b, v_b], axis=2))
        pfl.append(logf)

        (q_a, kv_c, kv_s, kv_w, g_nsa, q_b, k_b, v_b, f_b, gate_a, gate_b) = project(xs, g_mix[l], w_in_pad[l])
        kvc_all = jnp.concatenate([cache_nsa_cmp_kv[l, page_table].reshape(bs, past, 2, G_A, DH), kv_c], axis=1)
        o_cmp, sel = sample_cmp_select(q_a, pos_s, kvc_all[:, :, 0], kvc_all[:, :, 1], *cmp_w)
        w_all = jnp.concatenate([state_nsa_win_kv[l], kv_w], axis=1)
        logf = jax.nn.log_sigmoid(f_b.astype(f32) + b_forget[l].astype(f32))
        o_sel, o_win, o_b = sample_group_attention(l, q_a, kv_s, kv_w, q_b, k_b, v_b, logf, sel, cache_nsa_sel_kv,
                                                   cache_fox_kv, cache_fox_logf, state_nsa_win_kv, page_table,
                                                   rel_bias_table)
        xs = layer_tail(xs, o_cmp, o_sel, o_win, g_nsa, o_b, gate_a, gate_b, p_sample[l], lw, peer_w)
        sc.append(kv_c)
        ssel.append(kv_s)
        sw.append(w_all[:, ts:])
        sfk.append(jnp.stack([k_b, v_b], axis=2))
        sfl.append(logf)

    y_prompt = rms_norm(xp, g_final)
    y_sample = rms_norm(xs, g_final)
    return (y_prompt, y_sample,
            jnp.stack(pc), jnp.stack(psel), jnp.stack(pw), jnp.stack(pfk), jnp.stack(pfl),
            jnp.stack(sc), jnp.stack(ssel), jnp.stack(sw), jnp.stack(sfk), jnp.stack(sfl))
```

```python
import functools
import math

import jax
import jax.numpy as jnp
import numpy as np
from jax import lax
from jax.experimental import pallas as pl
from jax.experimental.pallas import tpu as pltpu

D_MODEL = 1024
PAGE_SIZE = 128
H_A = 8
G_A = 2
HG_A = H_A // G_A
DH = 64
L_CMP = 32
L_SEL = 64
N_SEL = 16
WINDOW = 512
H_B = 8
N_BUCKETS = 32
MAX_DISTANCE = 128
PEER_HEADS = 8
N_KEYS = 128
N_EXPERTS = N_KEYS * N_KEYS
PEER_TOPK = 16
D_KEY = 256
D_HALF = D_KEY // 2
PEER_TOKEN_BLOCK = 128
Q_BLOCK = 128
EPS = 1e-6
NEG = -1e30
FORCE = 1e4
SCALE = DH ** -0.5
W_A = H_A * DH
W_B = H_B * DH
SPLIT_SIZES = (W_A, 2 * G_A * DH, 2 * G_A * DH, 2 * G_A * DH, 3 * H_A,
               W_B, W_B, W_B, H_B, D_MODEL, D_MODEL)
N_IN = sum(SPLIT_SIZES)

LANES = 128
VMEM_LIMIT = 48 * 1024 * 1024
CDT = jnp.bfloat16
f32 = jnp.float32


def _nt_dot(a, b):
    return lax.dot_general(a, b, (((1,), (1,)), ((), ())), preferred_element_type=f32)


def _norm_proj_body(x_ref, g_ref, w_ref, o_ref):
    x = x_ref[...]
    ms = jnp.mean(x * x, axis=-1, keepdims=True)
    h = x * lax.rsqrt(ms + EPS) * g_ref[...]
    o_ref[...] = jnp.dot(h.astype(CDT), w_ref[...], preferred_element_type=f32)


def norm_proj(x2d, g, w, block_n=256):
    n, d = x2d.shape
    m = w.shape[1]
    return pl.pallas_call(
        _norm_proj_body,
        grid=(n // block_n,),
        in_specs=[pl.BlockSpec((block_n, d), lambda i: (i, 0)),
                  pl.BlockSpec((1, d), lambda i: (0, 0)),
                  pl.BlockSpec((d, m), lambda i: (0, 0))],
        out_specs=pl.BlockSpec((block_n, m), lambda i: (i, 0)),
        out_shape=jax.ShapeDtypeStruct((n, m), f32),
        compiler_params=pltpu.CompilerParams(dimension_semantics=("parallel",),
                                             vmem_limit_bytes=VMEM_LIMIT),
        name="norm_proj",
    )(x2d, g.reshape(1, d), w)


def _compress_body(a_ref, pe_ref, w_ref, o_ref):
    o_ref[...] = jnp.dot((a_ref[...] + pe_ref[...]).astype(CDT), w_ref[...], preferred_element_type=f32)


def nsa_compress(kv_c, w_ck, w_cv, pe, block_m=128):
    b, t = kv_c.shape[:2]
    nb_c = t // L_CMP
    row = L_CMP * 2 * G_A * DH
    eye = jnp.eye(2, dtype=f32)
    w_big = jnp.einsum('klde,kK,gG->lkgdKGe', jnp.stack([w_ck, w_cv]), eye, eye).reshape(row, 2 * G_A * DH).astype(CDT)
    pe_big = jnp.broadcast_to(pe[:, None, None, :], (L_CMP, 2, G_A, DH)).reshape(1, row)
    a = kv_c.reshape(b * nb_c, row)
    m = a.shape[0]
    out = pl.pallas_call(
        _compress_body,
        grid=(m // block_m,),
        in_specs=[pl.BlockSpec((block_m, row), lambda i: (i, 0)),
                  pl.BlockSpec((1, row), lambda i: (0, 0)),
                  pl.BlockSpec((row, 2 * G_A * DH), lambda i: (0, 0))],
        out_specs=pl.BlockSpec((block_m, 2 * G_A * DH), lambda i: (i, 0)),
        out_shape=jax.ShapeDtypeStruct((m, 2 * G_A * DH), f32),
        compiler_params=pltpu.CompilerParams(dimension_semantics=("parallel",), vmem_limit_bytes=VMEM_LIMIT),
        name="nsa_compress",
    )(a, pe_big, w_big)
    out = out.reshape(b, nb_c, 2, G_A, DH)
    return out[:, :, 0], out[:, :, 1]


def t5_bucket_np(dist):
    n = np.maximum(dist, 0)
    nf = np.maximum(n, 1).astype(np.float32)
    max_exact = N_BUCKETS // 2
    large = max_exact + (np.log(nf / np.float32(max_exact)) / np.float32(math.log(MAX_DISTANCE / max_exact))
                         * np.float32(N_BUCKETS - max_exact)).astype(np.int32)
    large = np.minimum(large, N_BUCKETS - 1)
    return np.where(n < max_exact, n, large).astype(np.int32)


def cmp_perm(nb_c):
    half = nb_c // 2
    npr = np.arange(nb_c)
    return 2 * (npr % half) + npr // half


def nsa_bias_tables(table, t):
    nb_c = t // L_CMP
    nq = t // Q_BLOCK
    tg = table.reshape(N_BUCKETS, G_A, HG_A)
    end = L_CMP * cmp_perm(nb_c) + (L_CMP - 1)
    tpos = np.arange(t).reshape(nq, 1, Q_BLOCK)
    bc = t5_bucket_np(tpos - end[None, :, None])
    bias_c = jnp.transpose(tg[bc], (3, 4, 0, 1, 2))
    i = np.arange(Q_BLOCK)
    d0 = i[None, :] - i[:, None]
    bs = np.stack([t5_bucket_np(d0), t5_bucket_np(d0 + Q_BLOCK),
                   np.full((Q_BLOCK, Q_BLOCK), N_BUCKETS - 1, np.int32)])
    bias_s = jnp.transpose(tg[bs], (3, 0, 4, 1, 2))
    return bias_c.astype(f32), bias_s.astype(f32)


def _softmax_cols(s, mask):
    s = jnp.where(mask, s, NEG)
    m = jnp.max(s, axis=0, keepdims=True)
    e = jnp.where(mask, jnp.exp(s - m), 0.0)
    l = jnp.sum(e, axis=0, keepdims=True)
    return e / jnp.where(l > 0.0, l, 1.0)


def _nsa_body(q_ref, kc_ref, vc_ref, bc_ref, ks_ref, vst_ref, kw_ref, vwt_ref, bs_ref, ocmp_ref, osel_ref, owin_ref,
              m_scr, l_scr, acc_scr, *, nb_c):
    qb = Q_BLOCK
    qi = pl.program_id(2)
    nb_s = nb_c // 2
    tpos = qi * qb + lax.broadcasted_iota(jnp.int32, (1, qb), 1)
    npr = lax.broadcasted_iota(jnp.int32, (nb_c, 1), 0)
    n_of = 2 * (npr % nb_s) + npr // nb_s
    mask_c = tpos >= (L_CMP * n_of + (L_CMP - 1))
    psum = jnp.zeros((nb_c, qb), f32)
    for r in range(HG_A):
        s_t = _nt_dot(kc_ref[0, 0], q_ref[0, 0, r]) + bc_ref[0, r, 0]
        p_t = _softmax_cols(s_t, mask_c)
        psum = psum + p_t
        o_t = jnp.dot(vc_ref[0, 0], p_t.astype(CDT), preferred_element_type=f32)
        ocmp_ref[0, :, r * DH:(r + 1) * DH] = o_t.T
    imp = psum[:nb_s] + psum[nb_s:]
    m_io = lax.broadcasted_iota(jnp.int32, (nb_s, qb), 0)
    cur = tpos // L_SEL
    imp = jnp.where((m_io == 0) | (m_io == cur), FORCE, imp)
    imp = jnp.where(m_io <= cur, imp, NEG)
    m_f = m_io.astype(f32)
    sel = jnp.zeros((nb_s, qb), f32)
    for _ in range(N_SEL):
        mx = jnp.max(imp, axis=0, keepdims=True)
        idx = jnp.min(jnp.where(imp == mx, m_f, float(nb_s)), axis=0, keepdims=True)
        hit = m_f == idx
        sel = jnp.where(hit, 1.0, sel)
        imp = jnp.where(hit, -jnp.inf, imp)
    sel_c = sel.astype(CDT)
    q4 = q_ref[0, 0].reshape(HG_A * qb, DH)
    key_half = lax.broadcasted_iota(jnp.int32, (qb, nb_s), 0) // L_SEL
    m_col = lax.broadcasted_iota(jnp.int32, (qb, nb_s), 1)
    j_io = lax.broadcasted_iota(jnp.int32, (qb, qb), 0)
    i_io = lax.broadcasted_iota(jnp.int32, (qb, qb), 1)

    def step(kb, bias_idx, k_ref, vt_ref, valid):
        k0 = pl.multiple_of(kb * qb, qb)
        s_t = _nt_dot(k_ref[0, 0, pl.ds(k0, qb), :], q4)
        for r in range(HG_A):
            cs = slice(r * qb, (r + 1) * qb)
            sr = s_t[:, cs] + bs_ref[0, bias_idx, r]
            if valid is not None:
                sr = jnp.where(valid, sr, NEG)
            m_old = m_scr[:, cs]
            m_new = jnp.maximum(m_old, jnp.max(sr, axis=0, keepdims=True))
            alpha = jnp.exp(m_old - m_new)
            p = jnp.exp(sr - m_new)
            l_scr[:, cs] = alpha * l_scr[:, cs] + jnp.sum(p, axis=0, keepdims=True)
            m_scr[:, cs] = m_new
            pv = jnp.dot(vt_ref[0, 0, kb], p.astype(CDT), preferred_element_type=f32)
            acc_scr[:, cs] = alpha * acc_scr[:, cs] + pv

    def reset():
        m_scr[...] = jnp.full(m_scr.shape, NEG, f32)
        l_scr[...] = jnp.zeros(l_scr.shape, f32)
        acc_scr[...] = jnp.zeros(acc_scr.shape, f32)

    def emit(o_ref):
        for r in range(HG_A):
            cs = slice(r * qb, (r + 1) * qb)
            o_ref[0, :, r * DH:(r + 1) * DH] = (acc_scr[:, cs] / l_scr[:, cs]).T

    def selected(kb):
        expand = jnp.where(m_col == 2 * kb + key_half, 1.0, 0.0).astype(CDT)
        return jnp.dot(expand, sel_c, preferred_element_type=f32) > 0.5

    reset()

    def far(kb, carry):
        step(kb, 2, ks_ref, vst_ref, selected(kb))
        return carry

    lax.fori_loop(0, jnp.maximum(qi - 1, 0), far, 0)

    @pl.when(qi > 0)
    def _():
        step(qi - 1, 1, ks_ref, vst_ref, selected(qi - 1))

    step(qi, 0, ks_ref, vst_ref, selected(qi) & (j_io <= i_io))
    emit(osel_ref)

    reset()
    n_back = WINDOW // qb
    for delta in range(n_back, 0, -1):
        @pl.when(qi >= delta)
        def _(delta=delta):
            step(qi - delta, 2 if delta > 1 else 1, kw_ref, vwt_ref, (j_io > i_io) if delta == n_back else None)

    step(qi, 0, kw_ref, vwt_ref, j_io <= i_io)
    emit(owin_ref)


def nsa_prompt(q_a, k_cmp, v_cmp, kv_s, kv_w, bias_c, bias_s):
    b, t, _ = q_a.shape
    qb = Q_BLOCK
    nb_c = t // L_CMP
    nq = t // qb
    perm = cmp_perm(nb_c)
    q = (q_a * SCALE).reshape(b, t, G_A, HG_A, DH).transpose(0, 2, 3, 1, 4).astype(CDT)
    kc = k_cmp[:, perm].transpose(0, 2, 1, 3).astype(CDT)
    vct = v_cmp[:, perm].transpose(0, 2, 3, 1).astype(CDT)
    ks = kv_s[:, :, 0].transpose(0, 2, 1, 3).astype(CDT)
    vst = kv_s[:, :, 1].reshape(b, nq, qb, G_A, DH).transpose(0, 3, 1, 4, 2).astype(CDT)
    kw = kv_w[:, :, 0].transpose(0, 2, 1, 3).astype(CDT)
    vwt = kv_w[:, :, 1].reshape(b, nq, qb, G_A, DH).transpose(0, 3, 1, 4, 2).astype(CDT)
    out = jax.ShapeDtypeStruct((b, t, H_A * DH), f32)
    ospec = pl.BlockSpec((1, qb, HG_A * DH), lambda bi, g, qi: (bi, qi, g))
    return pl.pallas_call(
        functools.partial(_nsa_body, nb_c=nb_c),
        grid=(b, G_A, nq),
        in_specs=[pl.BlockSpec((1, 1, HG_A, qb, DH), lambda bi, g, qi: (bi, g, 0, qi, 0)),
                  pl.BlockSpec((1, 1, nb_c, DH), lambda bi, g, qi: (bi, g, 0, 0)),
                  pl.BlockSpec((1, 1, DH, nb_c), lambda bi, g, qi: (bi, g, 0, 0)),
                  pl.BlockSpec((1, HG_A, 1, nb_c, qb), lambda bi, g, qi: (g, 0, qi, 0, 0)),
                  pl.BlockSpec((1, 1, t, DH), lambda bi, g, qi: (bi, g, 0, 0)),
                  pl.BlockSpec((1, 1, nq, DH, qb), lambda bi, g, qi: (bi, g, 0, 0, 0)),
                  pl.BlockSpec((1, 1, t, DH), lambda bi, g, qi: (bi, g, 0, 0)),
                  pl.BlockSpec((1, 1, nq, DH, qb), lambda bi, g, qi: (bi, g, 0, 0, 0)),
                  pl.BlockSpec((1, 3, HG_A, qb, qb), lambda bi, g, qi: (g, 0, 0, 0, 0))],
        out_specs=[ospec, ospec, ospec],
        out_shape=[out, out, out],
        scratch_shapes=[pltpu.VMEM((1, HG_A * qb), f32), pltpu.VMEM((1, HG_A * qb), f32),
                        pltpu.VMEM((DH, HG_A * qb), f32)],
        compiler_params=pltpu.CompilerParams(dimension_semantics=("parallel", "parallel", "arbitrary"),
                                             vmem_limit_bytes=VMEM_LIMIT),
        name="nsa_prompt",
    )(q, kc, vct, bias_c, ks, vst, kw, vwt, bias_s)


NO_RANK = 99.0
PEER_CHUNK = 1024
PEER_JT = 32


def _extract_topk(work, iota_f, n_rows, top_ref=None):
    rank = jnp.full(work.shape, NO_RANK, f32)
    for k in range(PEER_TOPK):
        m = jnp.max(work, axis=0, keepdims=True)
        idx = jnp.min(jnp.where(work == m, iota_f, float(n_rows)), axis=0, keepdims=True)
        hit = iota_f == idx
        rank = jnp.where(hit, float(k), rank)
        work = jnp.where(hit, -jnp.inf, work)
        if top_ref is not None:
            top_ref[k:k + 1, :] = m
    return rank


def _peer_route_body(x_ref, g_ref, wq_ref, sk_ref, hb_ref, r2_ref, b_ref, c_ref, a_ref,
                     s_scr, top1_scr, top2_scr, cand_scr):
    x = x_ref[...]
    tn = x.shape[0]
    n_cand = PEER_TOPK * PEER_TOPK
    ms = jnp.mean(x * x, axis=-1, keepdims=True)
    hb = (x * lax.rsqrt(ms + EPS) * g_ref[...]).astype(CDT)
    hb_ref[...] = hb
    iota_k = lax.broadcasted_iota(jnp.int32, (N_KEYS, tn), 0).astype(f32)
    iota_c = lax.broadcasted_iota(jnp.int32, (n_cand, tn), 0).astype(f32)
    for p in range(PEER_HEADS):
        ranks = []
        for half in range(2):
            r0 = p * D_KEY + half * D_HALF
            q_t = _nt_dot(wq_ref[r0:r0 + D_HALF, :], hb)
            s_t = jnp.dot(sk_ref[p * 2 + half], q_t.astype(CDT), preferred_element_type=f32)
            s_scr[half] = s_t
            ranks.append(_extract_topk(s_t, iota_k, N_KEYS, top1_scr if half == 0 else top2_scr))
        top1 = top1_scr[...]
        top2 = top2_scr[...]
        for a in range(PEER_TOPK):
            cand_scr[a * PEER_TOPK:(a + 1) * PEER_TOPK, :] = top1[a:a + 1, :] + top2
        rank_c = _extract_topk(cand_scr[...], iota_c, n_cand)
        picked = jnp.where(rank_c < NO_RANK, 1.0, 0.0)
        e1 = jnp.exp(top1 - top1[0:1, :])
        e2 = jnp.exp(top2 - top2[0:1, :])
        z = jnp.zeros((1, tn), f32)
        c_tile = jnp.zeros((N_KEYS, tn), f32)
        for a in range(PEER_TOPK):
            pk = picked[a * PEER_TOPK:(a + 1) * PEER_TOPK, :]
            cnt = jnp.sum(pk, axis=0, keepdims=True)
            z = z + e1[a:a + 1, :] * jnp.sum(pk * e2, axis=0, keepdims=True)
            c_tile = jnp.where(ranks[0] == float(a), cnt, c_tile)
        r2_ref[p] = ranks[1]
        c_ref[p] = c_tile
        a_ref[p] = jnp.exp(s_scr[0] - top1[0:1, :])
        b_ref[p] = jnp.exp(s_scr[1] - top2[0:1, :]) / z


def peer_route(x2d, g, wq_t, sk, block_n=128):
    n, d = x2d.shape
    tile = jax.ShapeDtypeStruct((PEER_HEADS, N_KEYS, n), f32)
    tspec = pl.BlockSpec((PEER_HEADS, N_KEYS, block_n), lambda i: (0, 0, i))
    return pl.pallas_call(
        _peer_route_body,
        grid=(n // block_n,),
        in_specs=[pl.BlockSpec((block_n, d), lambda i: (i, 0)),
                  pl.BlockSpec((1, d), lambda i: (0, 0)),
                  pl.BlockSpec(wq_t.shape, lambda i: (0, 0)),
                  pl.BlockSpec(sk.shape, lambda i: (0, 0, 0))],
        out_specs=[pl.BlockSpec((block_n, d), lambda i: (i, 0)), tspec, tspec, tspec, tspec],
        out_shape=[jax.ShapeDtypeStruct((n, d), CDT), tile, tile, tile, tile],
        scratch_shapes=[pltpu.VMEM((2, N_KEYS, block_n), f32),
                        pltpu.VMEM((PEER_TOPK, block_n), f32),
                        pltpu.VMEM((PEER_TOPK, block_n), f32),
                        pltpu.VMEM((PEER_TOPK * PEER_TOPK, block_n), f32)],
        compiler_params=pltpu.CompilerParams(dimension_semantics=("parallel",), vmem_limit_bytes=VMEM_LIMIT),
        name="peer_route",
    )(x2d, g.reshape(1, d), wq_t, sk)


def _gelu_tanh(x):
    return 0.5 * x * (1.0 + jnp.tanh(math.sqrt(2.0 / math.pi) * (x + 0.044715 * (x * x * x))))


def _peer_dense_body(x_ref, hb_ref, r2_ref, b_ref, c_ref, a_ref, u_ref, vt_ref, o_ref,
                     acc_ref, ga_ref):
    ch = pl.program_id(1)
    n_i = PEER_CHUNK // N_KEYS

    @pl.when(ch == 0)
    def _():
        acc_ref[...] = jnp.zeros_like(acc_ref)

    hb = hb_ref[...]
    for i in range(n_i):
        gi = ch * n_i + i
        act = _nt_dot(u_ref[i * N_KEYS:(i + 1) * N_KEYS, :], hb)
        c_rows = [c_ref[p, pl.ds(gi, 1), :] for p in range(PEER_HEADS)]
        a_rows = [a_ref[p, pl.ds(gi, 1), :] for p in range(PEER_HEADS)]
        for jt in range(N_KEYS // PEER_JT):
            js = slice(jt * PEER_JT, (jt + 1) * PEER_JT)
            g = None
            for p in range(PEER_HEADS):
                term = jnp.where(r2_ref[p, js, :] < c_rows[p], b_ref[p, js, :], 0.0) * a_rows[p]
                g = term if g is None else g + term
            row0 = i * N_KEYS + jt * PEER_JT
            ga_ref[row0:row0 + PEER_JT, :] = (_gelu_tanh(act[js, :]) * g).astype(CDT)
    acc_ref[...] += jnp.dot(vt_ref[...], ga_ref[...], preferred_element_type=f32)

    @pl.when(ch == pl.num_programs(1) - 1)
    def _():
        o_ref[...] = x_ref[...] + acc_ref[...].T


def peer_dense(x2d, hb, r2, b, c, a, u, vt, block_n=256):
    n, d = x2d.shape
    n_ch = N_EXPERTS // PEER_CHUNK
    tspec = pl.BlockSpec((PEER_HEADS, N_KEYS, block_n), lambda i, k: (0, 0, i))
    return pl.pallas_call(
        _peer_dense_body,
        grid=(n // block_n, n_ch),
        in_specs=[pl.BlockSpec((block_n, d), lambda i, k: (i, 0)),
                  pl.BlockSpec((block_n, d), lambda i, k: (i, 0)),
                  tspec, tspec, tspec, tspec,
                  pl.BlockSpec((PEER_CHUNK, d), lambda i, k: (k, 0)),
                  pl.BlockSpec((d, PEER_CHUNK), lambda i, k: (0, k))],
        out_specs=pl.BlockSpec((block_n, d), lambda i, k: (i, 0)),
        out_shape=jax.ShapeDtypeStruct((n, d), f32),
        scratch_shapes=[pltpu.VMEM((d, block_n), f32),
                        pltpu.VMEM((PEER_CHUNK, block_n), CDT)],
        compiler_params=pltpu.CompilerParams(dimension_semantics=("parallel", "arbitrary"),
                                             vmem_limit_bytes=VMEM_LIMIT),
        name="peer_dense",
    )(x2d, hb, r2, b, c, a, u, vt)


def peer_residual(x, g, peer_w):
    wq_t, sk, u_c, vt_c = peer_w
    shp = x.shape
    x2d = x.reshape(-1, shp[-1])
    hb, r2, b, c, a = peer_route(x2d, g, wq_t, sk)
    return peer_dense(x2d, hb, r2, b, c, a, u_c, vt_c).reshape(shp)


FOX_HB = 4


def _fox_body(q_ref, k_ref, vt_ref, dq_ref, dk_ref, o_ref, m_scr, l_scr, acc_scr):
    qb = Q_BLOCK
    qi = pl.program_id(2)
    m_scr[...] = jnp.full(m_scr.shape, NEG, f32)
    l_scr[...] = jnp.zeros(l_scr.shape, f32)
    acc_scr[...] = jnp.zeros(acc_scr.shape, f32)
    j_io = lax.broadcasted_iota(jnp.int32, (qb, qb), 0)
    i_io = lax.broadcasted_iota(jnp.int32, (qb, qb), 1)

    def step(kb, causal):
        k0 = pl.multiple_of(kb * qb, qb)
        for r in range(FOX_HB):
            cs = slice(r * qb, (r + 1) * qb)
            sr = _nt_dot(k_ref[0, r, pl.ds(k0, qb), :], q_ref[0, r])
            sr = sr + (dq_ref[0, r] - dk_ref[0, r, pl.ds(k0, qb), :])
            if causal:
                sr = jnp.where(j_io <= i_io, sr, NEG)
            m_old = m_scr[:, cs]
            m_new = jnp.maximum(m_old, jnp.max(sr, axis=0, keepdims=True))
            alpha = jnp.exp(m_old - m_new)
            p = jnp.exp(sr - m_new)
            l_scr[:, cs] = alpha * l_scr[:, cs] + jnp.sum(p, axis=0, keepdims=True)
            m_scr[:, cs] = m_new
            pv = jnp.dot(vt_ref[0, r, kb], p.astype(CDT), preferred_element_type=f32)
            acc_scr[:, cs] = alpha * acc_scr[:, cs] + pv

    def far(kb, carry):
        step(kb, False)
        return carry

    lax.fori_loop(0, qi, far, 0)
    step(qi, True)
    for r in range(FOX_HB):
        cs = slice(r * qb, (r + 1) * qb)
        o_ref[0, :, r * DH:(r + 1) * DH] = (acc_scr[:, cs] / l_scr[:, cs]).T


def fox_prompt(q_b, k_b, v_b, dcum):
    b, t = q_b.shape[:2]
    qb = Q_BLOCK
    nq = t // qb
    q = (q_b * SCALE).transpose(0, 2, 1, 3).astype(CDT)
    k = k_b.transpose(0, 2, 1, 3).astype(CDT)
    vt = v_b.reshape(b, nq, qb, H_B, DH).transpose(0, 3, 1, 4, 2).astype(CDT)
    d_t = dcum.transpose(0, 2, 1)
    dq = d_t.reshape(b, H_B, nq, 1, qb)
    dk = d_t.reshape(b, H_B, t, 1)
    return pl.pallas_call(
        _fox_body,
        grid=(b, H_B // FOX_HB, nq),
        in_specs=[pl.BlockSpec((1, FOX_HB, qb, DH), lambda bi, g, qi: (bi, g, qi, 0)),
                  pl.BlockSpec((1, FOX_HB, t, DH), lambda bi, g, qi: (bi, g, 0, 0)),
                  pl.BlockSpec((1, FOX_HB, nq, DH, qb), lambda bi, g, qi: (bi, g, 0, 0, 0)),
                  pl.BlockSpec((1, FOX_HB, None, 1, qb), lambda bi, g, qi: (bi, g, qi, 0, 0)),
                  pl.BlockSpec((1, FOX_HB, t, 1), lambda bi, g, qi: (bi, g, 0, 0))],
        out_specs=pl.BlockSpec((1, qb, FOX_HB * DH), lambda bi, g, qi: (bi, qi, g)),
        out_shape=jax.ShapeDtypeStruct((b, t, H_B * DH), f32),
        scratch_shapes=[pltpu.VMEM((1, FOX_HB * qb), f32), pltpu.VMEM((1, FOX_HB * qb), f32),
                        pltpu.VMEM((DH, FOX_HB * qb), f32)],
        compiler_params=pltpu.CompilerParams(dimension_semantics=("parallel", "parallel", "arbitrary"),
                                             vmem_limit_bytes=VMEM_LIMIT),
        name="fox_prompt",
    )(q, k, vt, dq, dk)


ROWS = 64
SEL_COLS = 40


def _online_update(s, v, m_ref, l_ref, acc_ref):
    m_old = m_ref[...]
    m_new = jnp.maximum(m_old, jnp.max(s, axis=-1, keepdims=True))
    alpha = jnp.exp(m_old - m_new)
    p = jnp.exp(s - m_new)
    l_ref[...] = alpha * l_ref[...] + jnp.sum(p, axis=-1, keepdims=True)
    m_ref[...] = m_new
    acc_ref[...] = alpha * acc_ref[...] + jnp.dot(p.astype(CDT), v, preferred_element_type=f32)


def _sample_body(pt_ref, qa_ref, qb_ref, mrow_ref, selc_ref, seln_ref, bsel_ref, foxc_ref, foxn_ref, fb_ref,
                 win_ref, winn_ref, bwin_ref, osel_ref, ofox_ref, owin_ref,
                 ms_ref, ls_ref, as_ref, mf_ref, lf_ref, af_ref, *, past):
    pg = pl.program_id(1)
    n_pg = pl.num_programs(1)
    ps = PAGE_SIZE
    w_a = G_A * DH
    w_b = H_B * DH
    ts = ROWS // H_A
    i_row = lax.broadcasted_iota(jnp.int32, (ROWS, ps), 0) % ts
    j_col = lax.broadcasted_iota(jnp.int32, (ROWS, ps), 1)
    qa = qa_ref[0]
    qb = qb_ref[0]

    @pl.when(pg == 0)
    def _():
        ms_ref[...] = jnp.full(ms_ref.shape, NEG, f32)
        ls_ref[...] = jnp.zeros(ls_ref.shape, f32)
        as_ref[...] = jnp.zeros(as_ref.shape, f32)
        mf_ref[...] = jnp.full(mf_ref.shape, NEG, f32)
        lf_ref[...] = jnp.zeros(lf_ref.shape, f32)
        af_ref[...] = jnp.zeros(af_ref.shape, f32)
        kw = win_ref[0, 0, :, 0:w_a].astype(CDT)
        vw = win_ref[0, 0, :, w_a:2 * w_a].astype(CDT)
        s1 = _nt_dot(qa, kw) + bwin_ref[:, 0:WINDOW]
        iw = lax.broadcasted_iota(jnp.int32, (ROWS, WINDOW), 0) % ts
        jw = lax.broadcasted_iota(jnp.int32, (ROWS, WINDOW), 1)
        s1 = jnp.where(jw > iw, s1, NEG)
        kn = winn_ref[0, :, 0:w_a].astype(CDT)
        vn = winn_ref[0, :, w_a:2 * w_a].astype(CDT)
        s2 = _nt_dot(qa, kn) + bwin_ref[:, WINDOW:WINDOW + ps]
        s2 = jnp.where(j_col <= i_row, s2, NEG)
        m = jnp.maximum(jnp.max(s1, axis=-1, keepdims=True), jnp.max(s2, axis=-1, keepdims=True))
        p1 = jnp.exp(s1 - m)
        p2 = jnp.exp(s2 - m)
        l = jnp.sum(p1, axis=-1, keepdims=True) + jnp.sum(p2, axis=-1, keepdims=True)
        o = (jnp.dot(p1.astype(CDT), vw, preferred_element_type=f32)
             + jnp.dot(p2.astype(CDT), vn, preferred_element_type=f32))
        owin_ref[0] = o / l

    causal = (pg * ps + j_col) <= (past + i_row)

    def sel_step(blk_ref):
        k = blk_ref[0, :, 0:w_a].astype(CDT)
        v = blk_ref[0, :, w_a:2 * w_a].astype(CDT)
        s = _nt_dot(qa, k) + bsel_ref[0]
        blk_of_key = 2 * pg + lax.broadcasted_iota(jnp.int32, (SEL_COLS, ps), 1) // L_SEL
        expand = jnp.where(lax.broadcasted_iota(jnp.int32, (SEL_COLS, ps), 0) == blk_of_key, 1.0, 0.0).astype(CDT)
        picked = jnp.dot(mrow_ref[0].astype(CDT), expand, preferred_element_type=f32) > 0.5
        s = jnp.where(picked & causal, s, NEG)
        _online_update(s, v, ms_ref, ls_ref, as_ref)

    def fox_step(blk_ref):
        k = blk_ref[0, :, 0:w_b].astype(CDT)
        v = blk_ref[0, :, w_b:2 * w_b].astype(CDT)
        s = _nt_dot(qb, k) + fb_ref[0, 0]
        s = jnp.where(causal, s, NEG)
        _online_update(s, v, mf_ref, lf_ref, af_ref)

    @pl.when(pg < n_pg - 1)
    def _():
        sel_step(selc_ref)
        fox_step(foxc_ref)

    @pl.when(pg == n_pg - 1)
    def _():
        sel_step(seln_ref)
        fox_step(foxn_ref)
        osel_ref[0] = as_ref[...] / ls_ref[...]
        ofox_ref[0] = af_ref[...] / lf_ref[...]


def sample_attention(pt_l, qa_bd, qb_bd, mrows, sel_cache2, sel_new, bias_sel, fox_cache2, fox_new, fbias,
                     win_state, win_new, bias_win, *, layer, past):
    bs, n_pages = pt_l.shape
    ps = PAGE_SIZE
    n_pg = n_pages + 1
    w_a = G_A * DH
    w_b = H_B * DH
    last = n_pages - 1
    grid_spec = pltpu.PrefetchScalarGridSpec(
        num_scalar_prefetch=1,
        grid=(bs, n_pg),
        in_specs=[
            pl.BlockSpec((1, ROWS, w_a), lambda b, p, pt: (b, 0, 0)),
            pl.BlockSpec((1, ROWS, w_b), lambda b, p, pt: (b, 0, 0)),
            pl.BlockSpec((1, ROWS, SEL_COLS), lambda b, p, pt: (b, 0, 0)),
            pl.BlockSpec((1, ps, 2 * w_a), lambda b, p, pt: (pt[b, jnp.minimum(p, last)], 0, 0)),
            pl.BlockSpec((1, ps, 2 * w_a), lambda b, p, pt: (b, 0, 0)),
            pl.BlockSpec((1, ROWS, ps), lambda b, p, pt: (p, 0, 0)),
            pl.BlockSpec((1, ps, 2 * w_b), lambda b, p, pt: (pt[b, jnp.minimum(p, last)], 0, 0)),
            pl.BlockSpec((1, ps, 2 * w_b), lambda b, p, pt: (b, 0, 0)),
            pl.BlockSpec((1, 1, ROWS, ps), lambda b, p, pt: (b, p, 0, 0)),
            pl.BlockSpec((1, 1, WINDOW, 2 * w_a), lambda b, p, pt: (layer, b, 0, 0)),
            pl.BlockSpec((1, ps, 2 * w_a), lambda b, p, pt: (b, 0, 0)),
            pl.BlockSpec((ROWS, WINDOW + ps), lambda b, p, pt: (0, 0)),
        ],
        out_specs=[pl.BlockSpec((1, ROWS, w_a), lambda b, p, pt: (b, 0, 0)),
                   pl.BlockSpec((1, ROWS, w_b), lambda b, p, pt: (b, 0, 0)),
                   pl.BlockSpec((1, ROWS, w_a), lambda b, p, pt: (b, 0, 0))],
        scratch_shapes=[pltpu.VMEM((ROWS, 1), f32), pltpu.VMEM((ROWS, 1), f32), pltpu.VMEM((ROWS, w_a), f32),
                        pltpu.VMEM((ROWS, 1), f32), pltpu.VMEM((ROWS, 1), f32), pltpu.VMEM((ROWS, w_b), f32)],
    )
    return pl.pallas_call(
        functools.partial(_sample_body, past=past),
        grid_spec=grid_spec,
        out_shape=[jax.ShapeDtypeStruct((bs, ROWS, w_a), f32), jax.ShapeDtypeStruct((bs, ROWS, w_b), f32),
                   jax.ShapeDtypeStruct((bs, ROWS, w_a), f32)],
        compiler_params=pltpu.CompilerParams(dimension_semantics=("parallel", "arbitrary"),
                                             vmem_limit_bytes=VMEM_LIMIT),
        name="sample_attention",
    )(pt_l, qa_bd, qb_bd, mrows, sel_cache2, sel_new, bias_sel, fox_cache2, fox_new, fbias, win_state, win_new,
      bias_win)


def sample_bias_tables(table, past, ts, n_pg):
    tg = table.reshape(N_BUCKETS, H_A)
    tq = past + np.arange(ts)
    s_abs = np.arange(n_pg * PAGE_SIZE)
    bsel = t5_bucket_np(tq[:, None] - s_abs[None, :])
    k_pos_w = np.concatenate([past - WINDOW + np.arange(WINDOW), past + np.arange(PAGE_SIZE)])
    bwin = t5_bucket_np(tq[:, None] - k_pos_w[None, :])

    def rows(bk):
        return jnp.transpose(tg[bk], (2, 0, 1)).reshape(H_A * ts, bk.shape[1])

    bias_sel = rows(bsel).reshape(ROWS, n_pg, PAGE_SIZE).transpose(1, 0, 2)
    return bias_sel.astype(f32), rows(bwin).astype(f32)


def block_diag_rows(q, heads_per_group):
    bs, ts, h, dh = q.shape
    n_c = h // heads_per_group
    qh = q.transpose(0, 2, 1, 3)
    onehot = (np.arange(h)[:, None] // heads_per_group == np.arange(n_c)[None, :]).astype(np.float32)
    out = qh[:, :, :, None, :] * onehot[None, :, None, :, None]
    return out.reshape(bs, h * ts, n_c * dh)


def diag_extract(o, heads_per_group, ts):
    bs = o.shape[0]
    h = o.shape[1] // ts
    n_c = h // heads_per_group
    o5 = o.reshape(bs, h, ts, n_c, DH)
    own = o5[:, np.arange(h), :, np.arange(h) // heads_per_group, :]
    return own.transpose(1, 2, 0, 3)


def sample_cmp_select(q_a, q_pos, k_cmp, v_cmp, table):
    b, nb_c = k_cmp.shape[:2]
    nb_s = nb_c * L_CMP // L_SEL
    n_sel = min(N_SEL, nb_s)
    end_pos = jnp.arange(nb_c) * L_CMP + (L_CMP - 1)
    tb = q_pos.shape[0]
    qg = q_a.reshape(b, tb, G_A, HG_A, DH)
    dist_c = q_pos[:, None] - end_pos[None, :]
    mask_c = dist_c >= 0
    s = jnp.einsum('btgrd,bngd->bgrtn', qg, k_cmp).astype(f32) * SCALE + rel_bias_grid(dist_c, table)
    p = jnp.where(mask_c, jax.nn.softmax(jnp.where(mask_c, s, NEG), axis=-1), 0.0)
    o_cmp = jnp.einsum('bgrtn,bngd->btgrd', p.astype(v_cmp.dtype), v_cmp)
    imp = p.sum(2).reshape(b, G_A, tb, nb_s, L_SEL // L_CMP).sum(-1)
    cur = (q_pos // L_SEL)[:, None]
    bidx = jnp.arange(nb_s)[None, :]
    imp = jnp.where((bidx == 0) | (bidx == cur), FORCE, imp)
    imp = jnp.where(bidx <= cur, imp, NEG)
    _, sel = lax.top_k(imp, n_sel)
    return o_cmp.reshape(b, tb, H_A, DH), sel


def sample_group_attention(l, q_a, kv_s, kv_w, q_b, k_b, v_b, logf, sel, cache_sel, cache_fox, cache_logf,
                           win_state, page_table, table):
    bs, ts = q_a.shape[:2]
    n_pages = page_table.shape[1]
    past = n_pages * PAGE_SIZE
    n_pool = cache_sel.shape[1]
    ps = PAGE_SIZE
    pt_l = (page_table + l * n_pool).astype(jnp.int32)
    qa_bd = block_diag_rows(q_a * SCALE, HG_A).astype(CDT)
    qb_bd = block_diag_rows(q_b * SCALE, 1).astype(CDT)
    onehot = jnp.sum(jax.nn.one_hot(sel, SEL_COLS, dtype=f32), axis=-2)
    mrows = jnp.broadcast_to(onehot[:, :, None], (bs, G_A, HG_A, ts, SEL_COLS)).reshape(bs, ROWS, SEL_COLS)
    padn = lambda a: jnp.pad(a.reshape(bs, ts, -1), ((0, 0), (0, ps - ts), (0, 0)))
    sel_new = padn(kv_s)
    win_new = padn(kv_w)
    fox_new = padn(jnp.stack([k_b, v_b], axis=2))
    bias_sel, bias_win = sample_bias_tables(table, past, ts, n_pages + 1)
    lf_past = cache_logf.reshape(cache_logf.shape[0], n_pool, ps * H_B)[l][page_table].reshape(bs, past, H_B)
    dk = jnp.cumsum(jnp.concatenate([lf_past, logf], axis=1), axis=1)
    dk_pad = jnp.pad(dk, ((0, 0), (0, ps - ts), (0, 0)))
    dq = dk[:, past:]
    fbias = dq.transpose(0, 2, 1)[:, :, :, None] - dk_pad.transpose(0, 2, 1)[:, :, None, :]
    fbias = fbias.reshape(bs, ROWS, n_pages + 1, ps).transpose(0, 2, 1, 3)
    sel_cache2 = cache_sel.reshape(-1, ps, 2 * G_A * DH)
    fox_cache2 = cache_fox.reshape(-1, ps, 2 * H_B * DH)
    win2 = win_state.reshape(win_state.shape[0], bs, WINDOW, 2 * G_A * DH)
    o_sel, o_fox, o_win = sample_attention(pt_l, qa_bd, qb_bd, mrows, sel_cache2, sel_new, bias_sel, fox_cache2,
                                           fox_new, fbias, win2, win_new, bias_win, layer=l, past=past)
    return diag_extract(o_sel, HG_A, ts), diag_extract(o_win, HG_A, ts), diag_extract(o_fox, 1, ts)


def rms_norm(x, g):
    x32 = x.astype(jnp.float32)
    y = x32 * lax.rsqrt(jnp.mean(x32 * x32, axis=-1, keepdims=True) + EPS)
    return (y * g.astype(jnp.float32)).astype(x.dtype)


def t5_bucket(dist):
    n = jnp.maximum(dist, 0)
    max_exact = N_BUCKETS // 2
    nf = jnp.maximum(n, 1).astype(jnp.float32)
    large = max_exact + (jnp.log(nf / max_exact) / math.log(MAX_DISTANCE / max_exact)
                         * (N_BUCKETS - max_exact)).astype(jnp.int32)
    large = jnp.minimum(large, N_BUCKETS - 1)
    return jnp.where(n < max_exact, n, large)


def rel_bias_grid(dist, table):
    b = table[t5_bucket(dist)].astype(jnp.float32)
    return jnp.moveaxis(b.reshape(*dist.shape, G_A, HG_A), (-2, -1), (0, 1))


def over_query_blocks(fn, q_pos, *qs):
    tq = q_pos.shape[0]
    if tq <= Q_BLOCK or tq % Q_BLOCK:
        return fn(q_pos, *qs)
    nb = tq // Q_BLOCK
    split = lambda a: jnp.moveaxis(a.reshape(a.shape[0], nb, Q_BLOCK, *a.shape[2:]), 1, 0)
    out = lax.map(lambda args: fn(*args), (q_pos.reshape(nb, Q_BLOCK),) + tuple(split(a) for a in qs))
    merge = lambda a: jnp.moveaxis(a, 0, 1).reshape(a.shape[1], tq, *a.shape[3:])
    return jax.tree_util.tree_map(merge, out)


def masked_attend(q, k, v, bias, mask):
    s = jnp.einsum('btgrd,bsgd->bgrts', q, k).astype(jnp.float32) * SCALE + bias
    p = jnp.where(mask, jax.nn.softmax(jnp.where(mask, s, NEG), axis=-1), 0.0)
    return jnp.einsum('bgrts,bsgd->btgrd', p.astype(v.dtype), v)


def window_attend(q, q_pos, k, v, k_pos, table):
    b, tq = q.shape[:2]
    dist = q_pos[:, None] - k_pos[None, :]
    mask = (dist >= 0) & (dist < WINDOW) & (k_pos[None, :] >= 0)
    o = masked_attend(q.reshape(b, tq, G_A, HG_A, DH), k, v, rel_bias_grid(dist, table), mask)
    return o.reshape(b, tq, H_A, DH)


def window_prompt(q, kw, vw, table):
    t = q.shape[1]
    padf = lambda a: jnp.pad(a, ((0, 0), (WINDOW, 0), (0, 0), (0, 0)))
    kp, vp = padf(kw), padf(vw)

    def blk(q_pos, qb):
        span = WINDOW + q_pos.shape[0]
        start = q_pos[0]
        kb = lax.dynamic_slice_in_dim(kp, start, span, axis=1)
        vb = lax.dynamic_slice_in_dim(vp, start, span, axis=1)
        k_pos = start - WINDOW + jnp.arange(span)
        return window_attend(qb, q_pos, kb, vb, k_pos, table)

    return over_query_blocks(blk, jnp.arange(t), q)


def nsa_cmp_sel(q, q_pos, kc, vc, ks, vs, w_ck, w_cv, pe, table):
    b, tk = kc.shape[:2]
    pad = (-tk) % L_SEL
    padt = lambda a: jnp.pad(a, ((0, 0), (0, pad), (0, 0), (0, 0)))
    kc, vc, ks, vs = padt(kc), padt(vc), padt(ks), padt(vs)
    tp = tk + pad
    nb_c, nb_s = tp // L_CMP, tp // L_SEL
    n_sel = min(N_SEL, nb_s)
    blocks = lambda a: a.reshape(b, nb_c, L_CMP, G_A, DH) + pe[:, None, :].astype(a.dtype)
    k_cmp = jnp.einsum('bnlgd,lde->bnge', blocks(kc), w_ck)
    v_cmp = jnp.einsum('bnlgd,lde->bnge', blocks(vc), w_cv)
    end_pos = jnp.arange(nb_c) * L_CMP + (L_CMP - 1)
    ks_b = jnp.moveaxis(ks.reshape(b, nb_s, L_SEL, G_A, DH), 3, 1)
    vs_b = jnp.moveaxis(vs.reshape(b, nb_s, L_SEL, G_A, DH), 3, 1)
    b_ix = jnp.arange(b)[:, None, None, None]
    g_ix = jnp.arange(G_A)[None, :, None, None]
    table_g = table.reshape(N_BUCKETS, G_A, HG_A)

    def blk(q_pos, qb):
        tb = q_pos.shape[0]
        qg = qb.reshape(b, tb, G_A, HG_A, DH)
        dist_c = q_pos[:, None] - end_pos[None, :]
        mask_c = dist_c >= 0
        s = jnp.einsum('btgrd,bngd->bgrtn', qg, k_cmp).astype(jnp.float32) * SCALE + rel_bias_grid(dist_c, table)
        p = jnp.where(mask_c, jax.nn.softmax(jnp.where(mask_c, s, NEG), axis=-1), 0.0)
        o_cmp = jnp.einsum('bgrtn,bngd->btgrd', p.astype(v_cmp.dtype), v_cmp)
        imp = p.sum(2).reshape(b, G_A, tb, nb_s, L_SEL // L_CMP).sum(-1)
        cur = (q_pos // L_SEL)[:, None]
        bidx = jnp.arange(nb_s)[None, :]
        imp = jnp.where((bidx == 0) | (bidx == cur), FORCE, imp)
        imp = jnp.where(bidx <= cur, imp, NEG)
        _, sel = lax.top_k(imp, n_sel)
        kg = ks_b[b_ix, g_ix, sel]
        vg = vs_b[b_ix, g_ix, sel]
        tok = sel[..., None] * L_SEL + jnp.arange(L_SEL)
        dist_s = q_pos[None, None, :, None, None] - tok
        bias_s = jnp.moveaxis(table_g[t5_bucket(dist_s), g_ix[..., None]], -1, 2).astype(jnp.float32)
        mask_s = (dist_s >= 0)[:, :, None]
        s2 = jnp.einsum('btgrd,bgtnld->bgrtnl', qg, kg).astype(jnp.float32) * SCALE + bias_s
        s2 = jnp.where(mask_s, s2, NEG).reshape(b, G_A, HG_A, tb, n_sel * L_SEL)
        p2 = jax.nn.softmax(s2, axis=-1)
        o_sel = jnp.einsum('bgrtk,bgtkd->btgrd', p2.astype(vg.dtype), vg.reshape(b, G_A, tb, n_sel * L_SEL, DH))
        return (o_cmp.reshape(b, tb, H_A, DH), o_sel.reshape(b, tb, H_A, DH))

    return over_query_blocks(blk, q_pos, q)


def fox_attend(q, dq, q_pos, k, v, dk, k_pos):
    b = q.shape[0]
    dk_t = jnp.moveaxis(dk, 2, 1)[:, :, None, None, :]

    def blk(q_pos, qb, dqb):
        tb = q_pos.shape[0]
        bias = jnp.moveaxis(dqb, 2, 1)[:, :, None, :, None] - dk_t
        mask = k_pos[None, :] <= q_pos[:, None]
        o = masked_attend(qb.reshape(b, tb, H_B, 1, DH), k, v, bias, mask)
        return o.reshape(b, tb, H_B, DH)

    return over_query_blocks(blk, q_pos, q, dq)


def project(x, g, w_in_pad):
    b, t = x.shape[:2]
    z = norm_proj(x.reshape(b * t, D_MODEL), g, w_in_pad)[:, :N_IN].reshape(b, t, N_IN)
    (q_a, kv_c, kv_s, kv_w, g_nsa, q_b, k_b, v_b, f_b, gate_a, gate_b) = jnp.split(
        z, np.cumsum(SPLIT_SIZES)[:-1].tolist(), axis=-1)
    kv = lambda a: a.reshape(b, t, 2, G_A, DH)
    hd = lambda a, nh: a.reshape(b, t, nh, DH)
    return (hd(q_a, H_A), kv(kv_c), kv(kv_s), kv(kv_w), g_nsa.reshape(b, t, 3, H_A),
            hd(q_b, H_B), hd(k_b, H_B), hd(v_b, H_B), f_b, gate_a, gate_b)


def layer_tail(x, o_cmp, o_sel, o_win, g_nsa, o_b, gate_a, gate_b, p_l, lw, peer_w):
    (w_pa, w_pb, w_o, g_f, g_p, w_pe, w_pg) = lw
    b, t = x.shape[:2]
    gn = jax.nn.sigmoid(g_nsa)[..., None]
    o_a = gn[:, :, 0] * o_cmp + gn[:, :, 1] * o_sel + gn[:, :, 2] * o_win
    mix = (jax.nn.sigmoid(gate_a) * (o_a.reshape(b, t, W_A) @ w_pa)
           + jax.nn.sigmoid(gate_b) * (o_b.reshape(b, t, W_B) @ w_pb))
    x = x + mix @ w_o
    x = peer_residual(x, g_f, peer_w)
    return x + (p_l @ w_pe) * jax.nn.sigmoid(rms_norm(x, g_p) @ w_pg)


def kernel(x_prompt, x_sample, cache_nsa_cmp_kv, cache_nsa_sel_kv, cache_fox_kv, cache_fox_logf,
           state_nsa_win_kv, page_table, p_prompt, p_sample, g_mix, w_in, b_forget, w_cmp_k, w_cmp_v,
           pe_cmp, rel_bias_table, w_proj_a, w_proj_b, w_out, g_ffn, peer_w_q, peer_sub_keys,
           peer_u, peer_v, g_ple, w_ple, w_ple_gate, g_final):
    depth = w_in.shape[0]
    bp, tp = x_prompt.shape[:2]
    bs, ts = x_sample.shape[:2]
    past = page_table.shape[1] * PAGE_SIZE
    win_p = min(WINDOW, tp)
    win_buf = state_nsa_win_kv.shape[2]
    pos_p = jnp.arange(tp)
    pos_s = past + jnp.arange(ts)
    pos_all = jnp.arange(past + ts)
    pos_w = past - win_buf + jnp.arange(win_buf + ts)
    n_in_pad = -(-N_IN // LANES) * LANES
    w_in_pad = jnp.pad(w_in, ((0, 0), (0, 0), (0, n_in_pad - N_IN))).astype(CDT)
    bias_c, bias_s = nsa_bias_tables(rel_bias_table, tp)
    xp, xs = x_prompt, x_sample
    pc, psel, pw, pfk, pfl = [], [], [], [], []
    sc, ssel, sw, sfk, sfl = [], [], [], [], []
    for l in range(depth):
        lw = (w_proj_a[l], w_proj_b[l], w_out[l], g_ffn[l], g_ple[l], w_ple[l], w_ple_gate[l])
        peer_w = (peer_w_q[l].T.astype(CDT),
                  peer_sub_keys[l].reshape(PEER_HEADS * 2, N_KEYS, D_HALF).astype(CDT),
                  peer_u[l].astype(CDT), peer_v[l].T.astype(CDT))
        cmp_w = (w_cmp_k[l], w_cmp_v[l], pe_cmp[l], rel_bias_table)

        (q_a, kv_c, kv_s, kv_w, g_nsa, q_b, k_b, v_b, f_b, gate_a, gate_b) = project(xp, g_mix[l], w_in_pad[l])
        k_cmp, v_cmp = nsa_compress(kv_c, w_cmp_k[l], w_cmp_v[l], pe_cmp[l])
        o_cmp, o_sel, o_win = nsa_prompt(q_a.reshape(bp, tp, W_A), k_cmp, v_cmp, kv_s, kv_w, bias_c, bias_s)
        o_cmp = o_cmp.reshape(bp, tp, H_A, DH)
        o_sel = o_sel.reshape(bp, tp, H_A, DH)
        o_win = o_win.reshape(bp, tp, H_A, DH)
        logf = jax.nn.log_sigmoid(f_b.astype(f32) + b_forget[l].astype(f32))
        dcum = jnp.cumsum(logf, axis=1)
        o_b = fox_prompt(q_b, k_b, v_b, dcum).reshape(bp, tp, H_B, DH)
        xp = layer_tail(xp, o_cmp, o_sel, o_win, g_nsa, o_b, gate_a, gate_b, p_prompt[l], lw, peer_w)
        pc.append(kv_c)
        psel.append(kv_s)
        pw.append(kv_w[:, tp - win_p:])
        pfk.append(jnp.stack([k_b, v_b], axis=2))
        pfl.append(logf)

        (q_a, kv_c, kv_s, kv_w, g_nsa, q_b, k_b, v_b, f_b, gate_a, gate_b) = project(xs, g_mix[l], w_in_pad[l])
        kc_pool, vc_pool = nsa_compress(cache_nsa_cmp_kv[l], *cmp_w[:3])
        kc_new, vc_new = nsa_compress(jnp.pad(kv_c, ((0, 0), (0, L_SEL - ts), (0, 0), (0, 0), (0, 0))), *cmp_w[:3])
        gather_pages = lambda a: a[page_table].reshape(bs, past // L_CMP, G_A, DH)
        k_cmp = jnp.concatenate([gather_pages(kc_pool), kc_new], axis=1)
        v_cmp = jnp.concatenate([gather_pages(vc_pool), vc_new], axis=1)
        o_cmp, sel = sample_cmp_select(q_a, pos_s, k_cmp, v_cmp, rel_bias_table)
        w_all = jnp.concatenate([state_nsa_win_kv[l], kv_w], axis=1)
        logf = jax.nn.log_sigmoid(f_b.astype(f32) + b_forget[l].astype(f32))
        o_sel, o_win, o_b = sample_group_attention(l, q_a, kv_s, kv_w, q_b, k_b, v_b, logf, sel, cache_nsa_sel_kv,
                                                   cache_fox_kv, cache_fox_logf, state_nsa_win_kv, page_table,
                                                   rel_bias_table)
        xs = layer_tail(xs, o_cmp, o_sel, o_win, g_nsa, o_b, gate_a, gate_b, p_sample[l], lw, peer_w)
        sc.append(kv_c)
        ssel.append(kv_s)
        sw.append(w_all[:, ts:])
        sfk.append(jnp.stack([k_b, v_b], axis=2))
        sfl.append(logf)

    y_prompt = rms_norm(xp, g_final)
    y_sample = rms_norm(xs, g_final)
    return (y_prompt, y_sample,
            jnp.stack(pc), jnp.stack(psel), jnp.stack(pw), jnp.stack(pfk), jnp.stack(pfl),
            jnp.stack(sc), jnp.stack(ssel), jnp.stack(sw), jnp.stack(sfk), jnp.stack(sfl))
```

```python
import functools
import math

import jax
import jax.numpy as jnp
import numpy as np
from jax import lax
from jax.experimental import pallas as pl
from jax.experimental.pallas import tpu as pltpu

D_MODEL = 1024
PAGE_SIZE = 128
H_A = 8
G_A = 2
HG_A = H_A // G_A
DH = 64
L_CMP = 32
L_SEL = 64
N_SEL = 16
WINDOW = 512
H_B = 8
N_BUCKETS = 32
MAX_DISTANCE = 128
PEER_HEADS = 8
N_KEYS = 128
N_EXPERTS = N_KEYS * N_KEYS
PEER_TOPK = 16
D_KEY = 256
D_HALF = D_KEY // 2
PEER_TOKEN_BLOCK = 128
Q_BLOCK = 128
EPS = 1e-6
NEG = -1e30
FORCE = 1e4
SCALE = DH ** -0.5
W_A = H_A * DH
W_B = H_B * DH
SPLIT_SIZES = (W_A, 2 * G_A * DH, 2 * G_A * DH, 2 * G_A * DH, 3 * H_A,
               W_B, W_B, W_B, H_B, D_MODEL, D_MODEL)
N_IN = sum(SPLIT_SIZES)

LANES = 128
VMEM_LIMIT = 48 * 1024 * 1024
CDT = jnp.bfloat16
f32 = jnp.float32


def _nt_dot(a, b):
    return lax.dot_general(a, b, (((1,), (1,)), ((), ())), preferred_element_type=f32)


def _norm_proj_body(x_ref, g_ref, w_ref, o_ref):
    x = x_ref[...]
    ms = jnp.mean(x * x, axis=-1, keepdims=True)
    h = x * lax.rsqrt(ms + EPS) * g_ref[...]
    o_ref[...] = jnp.dot(h.astype(CDT), w_ref[...], preferred_element_type=f32)


def norm_proj(x2d, g, w, block_n=256):
    n, d = x2d.shape
    m = w.shape[1]
    return pl.pallas_call(
        _norm_proj_body,
        grid=(n // block_n,),
        in_specs=[pl.BlockSpec((block_n, d), lambda i: (i, 0)),
                  pl.BlockSpec((1, d), lambda i: (0, 0)),
                  pl.BlockSpec((d, m), lambda i: (0, 0))],
        out_specs=pl.BlockSpec((block_n, m), lambda i: (i, 0)),
        out_shape=jax.ShapeDtypeStruct((n, m), f32),
        compiler_params=pltpu.CompilerParams(dimension_semantics=("parallel",),
                                             vmem_limit_bytes=VMEM_LIMIT),
        name="norm_proj",
    )(x2d, g.reshape(1, d), w)


def _compress_body(a_ref, pe_ref, w_ref, o_ref):
    o_ref[...] = jnp.dot((a_ref[...] + pe_ref[...]).astype(CDT), w_ref[...], preferred_element_type=f32)


def nsa_compress(kv_c, w_ck, w_cv, pe, block_m=128):
    b, t = kv_c.shape[:2]
    nb_c = t // L_CMP
    row = L_CMP * 2 * G_A * DH
    eye = jnp.eye(2, dtype=f32)
    w_big = jnp.einsum('klde,kK,gG->lkgdKGe', jnp.stack([w_ck, w_cv]), eye, eye).reshape(row, 2 * G_A * DH).astype(CDT)
    pe_big = jnp.broadcast_to(pe[:, None, None, :], (L_CMP, 2, G_A, DH)).reshape(1, row)
    a = kv_c.reshape(b * nb_c, row)
    m = a.shape[0]
    out = pl.pallas_call(
        _compress_body,
        grid=(m // block_m,),
        in_specs=[pl.BlockSpec((block_m, row), lambda i: (i, 0)),
                  pl.BlockSpec((1, row), lambda i: (0, 0)),
                  pl.BlockSpec((row, 2 * G_A * DH), lambda i: (0, 0))],
        out_specs=pl.BlockSpec((block_m, 2 * G_A * DH), lambda i: (i, 0)),
        out_shape=jax.ShapeDtypeStruct((m, 2 * G_A * DH), f32),
        compiler_params=pltpu.CompilerParams(dimension_semantics=("parallel",), vmem_limit_bytes=VMEM_LIMIT),
        name="nsa_compress",
    )(a, pe_big, w_big)
    out = out.reshape(b, nb_c, 2, G_A, DH)
    return out[:, :, 0], out[:, :, 1]


def t5_bucket_np(dist):
    n = np.maximum(dist, 0)
    nf = np.maximum(n, 1).astype(np.float32)
    max_exact = N_BUCKETS // 2
    large = max_exact + (np.log(nf / np.float32(max_exact)) / np.float32(math.log(MAX_DISTANCE / max_exact))
                         * np.float32(N_BUCKETS - max_exact)).astype(np.int32)
    large = np.minimum(large, N_BUCKETS - 1)
    return np.where(n < max_exact, n, large).astype(np.int32)


def cmp_perm(nb_c):
    half = nb_c // 2
    npr = np.arange(nb_c)
    return 2 * (npr % half) + npr // half


def nsa_bias_tables(table, t):
    nb_c = t // L_CMP
    nq = t // Q_BLOCK
    tg = table.reshape(N_BUCKETS, G_A, HG_A)
    end = L_CMP * cmp_perm(nb_c) + (L_CMP - 1)
    tpos = np.arange(t).reshape(nq, 1, Q_BLOCK)
    bc = t5_bucket_np(tpos - end[None, :, None])
    bias_c = jnp.transpose(tg[bc], (3, 4, 0, 1, 2))
    i = np.arange(Q_BLOCK)
    d0 = i[None, :] - i[:, None]
    bs = np.stack([t5_bucket_np(d0), t5_bucket_np(d0 + Q_BLOCK),
                   np.full((Q_BLOCK, Q_BLOCK), N_BUCKETS - 1, np.int32)])
    bias_s = jnp.transpose(tg[bs], (3, 0, 4, 1, 2))
    return bias_c.astype(f32), bias_s.astype(f32)


def _softmax_cols(s, mask):
    s = jnp.where(mask, s, NEG)
    m = jnp.max(s, axis=0, keepdims=True)
    e = jnp.where(mask, jnp.exp(s - m), 0.0)
    l = jnp.sum(e, axis=0, keepdims=True)
    return e / jnp.where(l > 0.0, l, 1.0)


def _nsa_body(q_ref, kc_ref, vc_ref, bc_ref, ks_ref, vst_ref, kw_ref, vwt_ref, bs_ref, ocmp_ref, osel_ref, owin_ref,
              m_scr, l_scr, acc_scr, *, nb_c):
    qb = Q_BLOCK
    qi = pl.program_id(2)
    nb_s = nb_c // 2
    tpos = qi * qb + lax.broadcasted_iota(jnp.int32, (1, qb), 1)
    npr = lax.broadcasted_iota(jnp.int32, (nb_c, 1), 0)
    n_of = 2 * (npr % nb_s) + npr // nb_s
    mask_c = tpos >= (L_CMP * n_of + (L_CMP - 1))
    psum = jnp.zeros((nb_c, qb), f32)
    for r in range(HG_A):
        s_t = _nt_dot(kc_ref[0, 0], q_ref[0, 0, r]) + bc_ref[0, r, 0]
        p_t = _softmax_cols(s_t, mask_c)
        psum = psum + p_t
        o_t = jnp.dot(vc_ref[0, 0], p_t.astype(CDT), preferred_element_type=f32)
        ocmp_ref[0, :, r * DH:(r + 1) * DH] = o_t.T
    imp = psum[:nb_s] + psum[nb_s:]
    m_io = lax.broadcasted_iota(jnp.int32, (nb_s, qb), 0)
    cur = tpos // L_SEL
    imp = jnp.where((m_io == 0) | (m_io == cur), FORCE, imp)
    imp = jnp.where(m_io <= cur, imp, NEG)
    m_f = m_io.astype(f32)
    sel = jnp.zeros((nb_s, qb), f32)
    for _ in range(N_SEL):
        mx = jnp.max(imp, axis=0, keepdims=True)
        idx = jnp.min(jnp.where(imp == mx, m_f, float(nb_s)), axis=0, keepdims=True)
        hit = m_f == idx
        sel = jnp.where(hit, 1.0, sel)
        imp = jnp.where(hit, -jnp.inf, imp)
    sel_c = sel.astype(CDT)
    q4 = q_ref[0, 0].reshape(HG_A * qb, DH)
    key_half = lax.broadcasted_iota(jnp.int32, (qb, nb_s), 0) // L_SEL
    m_col = lax.broadcasted_iota(jnp.int32, (qb, nb_s), 1)
    j_io = lax.broadcasted_iota(jnp.int32, (qb, qb), 0)
    i_io = lax.broadcasted_iota(jnp.int32, (qb, qb), 1)

    def step(kb, bias_idx, k_ref, vt_ref, valid):
        k0 = pl.multiple_of(kb * qb, qb)
        s_t = _nt_dot(k_ref[0, 0, pl.ds(k0, qb), :], q4)
        for r in range(HG_A):
            cs = slice(r * qb, (r + 1) * qb)
            sr = s_t[:, cs] + bs_ref[0, bias_idx, r]
            if valid is not None:
                sr = jnp.where(valid, sr, NEG)
            m_old = m_scr[:, cs]
            m_new = jnp.maximum(m_old, jnp.max(sr, axis=0, keepdims=True))
            alpha = jnp.exp(m_old - m_new)
            p = jnp.exp(sr - m_new)
            l_scr[:, cs] = alpha * l_scr[:, cs] + jnp.sum(p, axis=0, keepdims=True)
            m_scr[:, cs] = m_new
            pv = jnp.dot(vt_ref[0, 0, kb], p.astype(CDT), preferred_element_type=f32)
            acc_scr[:, cs] = alpha * acc_scr[:, cs] + pv

    def reset():
        m_scr[...] = jnp.full(m_scr.shape, NEG, f32)
        l_scr[...] = jnp.zeros(l_scr.shape, f32)
        acc_scr[...] = jnp.zeros(acc_scr.shape, f32)

    def emit(o_ref):
        for r in range(HG_A):
            cs = slice(r * qb, (r + 1) * qb)
            o_ref[0, :, r * DH:(r + 1) * DH] = (acc_scr[:, cs] / l_scr[:, cs]).T

    def selected(kb):
        expand = jnp.where(m_col == 2 * kb + key_half, 1.0, 0.0).astype(CDT)
        return jnp.dot(expand, sel_c, preferred_element_type=f32) > 0.5

    reset()

    def far(kb, carry):
        step(kb, 2, ks_ref, vst_ref, selected(kb))
        return carry

    lax.fori_loop(0, jnp.maximum(qi - 1, 0), far, 0)

    @pl.when(qi > 0)
    def _():
        step(qi - 1, 1, ks_ref, vst_ref, selected(qi - 1))

    step(qi, 0, ks_ref, vst_ref, selected(qi) & (j_io <= i_io))
    emit(osel_ref)

    reset()
    n_back = WINDOW // qb
    for delta in range(n_back, 0, -1):
        @pl.when(qi >= delta)
        def _(delta=delta):
            step(qi - delta, 2 if delta > 1 else 1, kw_ref, vwt_ref, (j_io > i_io) if delta == n_back else None)

    step(qi, 0, kw_ref, vwt_ref, j_io <= i_io)
    emit(owin_ref)


def nsa_prompt(q_a, k_cmp, v_cmp, kv_s, kv_w, bias_c, bias_s):
    b, t, _ = q_a.shape
    qb = Q_BLOCK
    nb_c = t // L_CMP
    nq = t // qb
    perm = cmp_perm(nb_c)
    q = (q_a * SCALE).reshape(b, t, G_A, HG_A, DH).transpose(0, 2, 3, 1, 4).astype(CDT)
    kc = k_cmp[:, perm].transpose(0, 2, 1, 3).astype(CDT)
    vct = v_cmp[:, perm].transpose(0, 2, 3, 1).astype(CDT)
    ks = kv_s[:, :, 0].transpose(0, 2, 1, 3).astype(CDT)
    vst = kv_s[:, :, 1].reshape(b, nq, qb, G_A, DH).transpose(0, 3, 1, 4, 2).astype(CDT)
    kw = kv_w[:, :, 0].transpose(0, 2, 1, 3).astype(CDT)
    vwt = kv_w[:, :, 1].reshape(b, nq, qb, G_A, DH).transpose(0, 3, 1, 4, 2).astype(CDT)
    out = jax.ShapeDtypeStruct((b, t, H_A * DH), f32)
    ospec = pl.BlockSpec((1, qb, HG_A * DH), lambda bi, g, qi: (bi, qi, g))
    return pl.pallas_call(
        functools.partial(_nsa_body, nb_c=nb_c),
        grid=(b, G_A, nq),
        in_specs=[pl.BlockSpec((1, 1, HG_A, qb, DH), lambda bi, g, qi: (bi, g, 0, qi, 0)),
                  pl.BlockSpec((1, 1, nb_c, DH), lambda bi, g, qi: (bi, g, 0, 0)),
                  pl.BlockSpec((1, 1, DH, nb_c), lambda bi, g, qi: (bi, g, 0, 0)),
                  pl.BlockSpec((1, HG_A, 1, nb_c, qb), lambda bi, g, qi: (g, 0, qi, 0, 0)),
                  pl.BlockSpec((1, 1, t, DH), lambda bi, g, qi: (bi, g, 0, 0)),
                  pl.BlockSpec((1, 1, nq, DH, qb), lambda bi, g, qi: (bi, g, 0, 0, 0)),
                  pl.BlockSpec((1, 1, t, DH), lambda bi, g, qi: (bi, g, 0, 0)),
                  pl.BlockSpec((1, 1, nq, DH, qb), lambda bi, g, qi: (bi, g, 0, 0, 0)),
                  pl.BlockSpec((1, 3, HG_A, qb, qb), lambda bi, g, qi: (g, 0, 0, 0, 0))],
        out_specs=[ospec, ospec, ospec],
        out_shape=[out, out, out],
        scratch_shapes=[pltpu.VMEM((1, HG_A * qb), f32), pltpu.VMEM((1, HG_A * qb), f32),
                        pltpu.VMEM((DH, HG_A * qb), f32)],
        compiler_params=pltpu.CompilerParams(dimension_semantics=("parallel", "parallel", "arbitrary"),
                                             vmem_limit_bytes=VMEM_LIMIT),
        name="nsa_prompt",
    )(q, kc, vct, bias_c, ks, vst, kw, vwt, bias_s)


NO_RANK = 99.0
PEER_CHUNK = 1024
PEER_JT = 32


def _extract_topk(work, iota_f, n_rows, top_ref=None):
    rank = jnp.full(work.shape, NO_RANK, f32)
    for k in range(PEER_TOPK):
        m = jnp.max(work, axis=0, keepdims=True)
        idx = jnp.min(jnp.where(work == m, iota_f, float(n_rows)), axis=0, keepdims=True)
        hit = iota_f == idx
        rank = jnp.where(hit, float(k), rank)
        work = jnp.where(hit, -jnp.inf, work)
        if top_ref is not None:
            top_ref[k:k + 1, :] = m
    return rank


def _peer_route_body(x_ref, g_ref, wq_ref, sk_ref, hb_ref, r2_ref, b_ref, c_ref, a_ref,
                     s_scr, top1_scr, top2_scr, cand_scr):
    x = x_ref[...]
    tn = x.shape[0]
    n_cand = PEER_TOPK * PEER_TOPK
    ms = jnp.mean(x * x, axis=-1, keepdims=True)
    hb = (x * lax.rsqrt(ms + EPS) * g_ref[...]).astype(CDT)
    hb_ref[...] = hb
    iota_k = lax.broadcasted_iota(jnp.int32, (N_KEYS, tn), 0).astype(f32)
    iota_c = lax.broadcasted_iota(jnp.int32, (n_cand, tn), 0).astype(f32)
    for p in range(PEER_HEADS):
        ranks = []
        for half in range(2):
            r0 = p * D_KEY + half * D_HALF
            q_t = _nt_dot(wq_ref[r0:r0 + D_HALF, :], hb)
            s_t = jnp.dot(sk_ref[p * 2 + half], q_t.astype(CDT), preferred_element_type=f32)
            s_scr[half] = s_t
            ranks.append(_extract_topk(s_t, iota_k, N_KEYS, top1_scr if half == 0 else top2_scr))
        top1 = top1_scr[...]
        top2 = top2_scr[...]
        for a in range(PEER_TOPK):
            cand_scr[a * PEER_TOPK:(a + 1) * PEER_TOPK, :] = top1[a:a + 1, :] + top2
        rank_c = _extract_topk(cand_scr[...], iota_c, n_cand)
        picked = jnp.where(rank_c < NO_RANK, 1.0, 0.0)
        e1 = jnp.exp(top1 - top1[0:1, :])
        e2 = jnp.exp(top2 - top2[0:1, :])
        z = jnp.zeros((1, tn), f32)
        c_tile = jnp.zeros((N_KEYS, tn), f32)
        for a in range(PEER_TOPK):
            pk = picked[a * PEER_TOPK:(a + 1) * PEER_TOPK, :]
            cnt = jnp.sum(pk, axis=0, keepdims=True)
            z = z + e1[a:a + 1, :] * jnp.sum(pk * e2, axis=0, keepdims=True)
            c_tile = jnp.where(ranks[0] == float(a), cnt, c_tile)
        r2_ref[p] = ranks[1]
        c_ref[p] = c_tile
        a_ref[p] = jnp.exp(s_scr[0] - top1[0:1, :])
        b_ref[p] = jnp.exp(s_scr[1] - top2[0:1, :]) / z


def peer_route(x2d, g, wq_t, sk, block_n=128):
    n, d = x2d.shape
    tile = jax.ShapeDtypeStruct((PEER_HEADS, N_KEYS, n), f32)
    tspec = pl.BlockSpec((PEER_HEADS, N_KEYS, block_n), lambda i: (0, 0, i))
    return pl.pallas_call(
        _peer_route_body,
        grid=(n // block_n,),
        in_specs=[pl.BlockSpec((block_n, d), lambda i: (i, 0)),
                  pl.BlockSpec((1, d), lambda i: (0, 0)),
                  pl.BlockSpec(wq_t.shape, lambda i: (0, 0)),
                  pl.BlockSpec(sk.shape, lambda i: (0, 0, 0))],
        out_specs=[pl.BlockSpec((block_n, d), lambda i: (i, 0)), tspec, tspec, tspec, tspec],
        out_shape=[jax.ShapeDtypeStruct((n, d), CDT), tile, tile, tile, tile],
        scratch_shapes=[pltpu.VMEM((2, N_KEYS, block_n), f32),
                        pltpu.VMEM((PEER_TOPK, block_n), f32),
                        pltpu.VMEM((PEER_TOPK, block_n), f32),
                        pltpu.VMEM((PEER_TOPK * PEER_TOPK, block_n), f32)],
        compiler_params=pltpu.CompilerParams(dimension_semantics=("parallel",), vmem_limit_bytes=VMEM_LIMIT),
        name="peer_route",
    )(x2d, g.reshape(1, d), wq_t, sk)


def _gelu_tanh(x):
    return 0.5 * x * (1.0 + jnp.tanh(math.sqrt(2.0 / math.pi) * (x + 0.044715 * (x * x * x))))


def _peer_dense_body(x_ref, hb_ref, r2_ref, b_ref, c_ref, a_ref, u_ref, vt_ref, o_ref,
                     acc_ref, ga_ref):
    ch = pl.program_id(1)
    n_i = PEER_CHUNK // N_KEYS

    @pl.when(ch == 0)
    def _():
        acc_ref[...] = jnp.zeros_like(acc_ref)

    hb = hb_ref[...]
    for i in range(n_i):
        gi = ch * n_i + i
        act = _nt_dot(u_ref[i * N_KEYS:(i + 1) * N_KEYS, :], hb)
        c_rows = [c_ref[p, pl.ds(gi, 1), :] for p in range(PEER_HEADS)]
        a_rows = [a_ref[p, pl.ds(gi, 1), :] for p in range(PEER_HEADS)]
        for jt in range(N_KEYS // PEER_JT):
            js = slice(jt * PEER_JT, (jt + 1) * PEER_JT)
            g = None
            for p in range(PEER_HEADS):
                term = jnp.where(r2_ref[p, js, :] < c_rows[p], b_ref[p, js, :], 0.0) * a_rows[p]
                g = term if g is None else g + term
            row0 = i * N_KEYS + jt * PEER_JT
            ga_ref[row0:row0 + PEER_JT, :] = (_gelu_tanh(act[js, :]) * g).astype(CDT)
    acc_ref[...] += jnp.dot(vt_ref[...], ga_ref[...], preferred_element_type=f32)

    @pl.when(ch == pl.num_programs(1) - 1)
    def _():
        o_ref[...] = x_ref[...] + acc_ref[...].T


def peer_dense(x2d, hb, r2, b, c, a, u, vt, block_n=256):
    n, d = x2d.shape
    n_ch = N_EXPERTS // PEER_CHUNK
    tspec = pl.BlockSpec((PEER_HEADS, N_KEYS, block_n), lambda i, k: (0, 0, i))
    return pl.pallas_call(
        _peer_dense_body,
        grid=(n // block_n, n_ch),
        in_specs=[pl.BlockSpec((block_n, d), lambda i, k: (i, 0)),
                  pl.BlockSpec((block_n, d), lambda i, k: (i, 0)),
                  tspec, tspec, tspec, tspec,
                  pl.BlockSpec((PEER_CHUNK, d), lambda i, k: (k, 0)),
                  pl.BlockSpec((d, PEER_CHUNK), lambda i, k: (0, k))],
        out_specs=pl.BlockSpec((block_n, d), lambda i, k: (i, 0)),
        out_shape=jax.ShapeDtypeStruct((n, d), f32),
        scratch_shapes=[pltpu.VMEM((d, block_n), f32),
                        pltpu.VMEM((PEER_CHUNK, block_n), CDT)],
        compiler_params=pltpu.CompilerParams(dimension_semantics=("parallel", "arbitrary"),
                                             vmem_limit_bytes=VMEM_LIMIT),
        name="peer_dense",
    )(x2d, hb, r2, b, c, a, u, vt)


def peer_residual(x, g, peer_w):
    wq_t, sk, u_c, vt_c = peer_w
    shp = x.shape
    x2d = x.reshape(-1, shp[-1])
    hb, r2, b, c, a = peer_route(x2d, g, wq_t, sk)
    return peer_dense(x2d, hb, r2, b, c, a, u_c, vt_c).reshape(shp)


FOX_HB = 4


def _fox_body(q_ref, k_ref, vt_ref, dq_ref, dk_ref, o_ref, m_scr, l_scr, acc_scr):
    qb = Q_BLOCK
    qi = pl.program_id(2)
    m_scr[...] = jnp.full(m_scr.shape, NEG, f32)
    l_scr[...] = jnp.zeros(l_scr.shape, f32)
    acc_scr[...] = jnp.zeros(acc_scr.shape, f32)
    j_io = lax.broadcasted_iota(jnp.int32, (qb, FOX_HB * qb), 0)
    i_io = lax.broadcasted_iota(jnp.int32, (qb, FOX_HB * qb), 1) % qb
    dq = dq_ref[0, 0]

    def step(kb, causal):
        k0 = pl.multiple_of(kb * qb, qb)
        s = jnp.concatenate([_nt_dot(k_ref[0, r, pl.ds(k0, qb), :], q_ref[0, r]) for r in range(FOX_HB)], axis=1)
        s = s + (dq - dk_ref[0, pl.ds(k0, qb), :])
        if causal:
            s = jnp.where(j_io <= i_io, s, NEG)
        m_old = m_scr[...]
        m_new = jnp.maximum(m_old, jnp.max(s, axis=0, keepdims=True))
        alpha = jnp.exp(m_old - m_new)
        p = jnp.exp(s - m_new)
        l_scr[...] = alpha * l_scr[...] + jnp.sum(p, axis=0, keepdims=True)
        m_scr[...] = m_new
        p = p.astype(CDT)
        pv = jnp.concatenate([jnp.dot(vt_ref[0, r, kb], p[:, r * qb:(r + 1) * qb], preferred_element_type=f32)
                              for r in range(FOX_HB)], axis=1)
        acc_scr[...] = alpha * acc_scr[...] + pv

    def far(kb, carry):
        step(kb, False)
        return carry

    lax.fori_loop(0, qi, far, 0)
    step(qi, True)
    o = acc_scr[...] / l_scr[...]
    for r in range(FOX_HB):
        o_ref[0, :, r * DH:(r + 1) * DH] = o[:, r * qb:(r + 1) * qb].T


def fox_prompt(q_b, k_b, v_b, dcum):
    b, t = q_b.shape[:2]
    qb = Q_BLOCK
    nq = t // qb
    q = (q_b * SCALE).transpose(0, 2, 1, 3).astype(CDT)
    k = k_b.transpose(0, 2, 1, 3).astype(CDT)
    vt = v_b.reshape(b, nq, qb, H_B, DH).transpose(0, 3, 1, 4, 2).astype(CDT)
    ng = H_B // FOX_HB
    d_t = dcum.transpose(0, 2, 1)
    dq = d_t.reshape(b, ng, FOX_HB, nq, qb).transpose(0, 1, 3, 2, 4).reshape(b, ng, nq, 1, FOX_HB * qb)
    dk = jnp.broadcast_to(d_t.reshape(b, ng, FOX_HB, t, 1), (b, ng, FOX_HB, t, qb)).transpose(0, 1, 3, 2, 4)
    dk = dk.reshape(b * ng, t, FOX_HB * qb)
    return pl.pallas_call(
        _fox_body,
        grid=(b, ng, nq),
        in_specs=[pl.BlockSpec((1, FOX_HB, qb, DH), lambda bi, g, qi: (bi, g, qi, 0)),
                  pl.BlockSpec((1, FOX_HB, t, DH), lambda bi, g, qi: (bi, g, 0, 0)),
                  pl.BlockSpec((1, FOX_HB, nq, DH, qb), lambda bi, g, qi: (bi, g, 0, 0, 0)),
                  pl.BlockSpec((1, 1, None, 1, FOX_HB * qb), lambda bi, g, qi: (bi, g, qi, 0, 0)),
                  pl.BlockSpec((1, t, FOX_HB * qb), lambda bi, g, qi: (bi * ng + g, 0, 0))],
        out_specs=pl.BlockSpec((1, qb, FOX_HB * DH), lambda bi, g, qi: (bi, qi, g)),
        out_shape=jax.ShapeDtypeStruct((b, t, H_B * DH), f32),
        scratch_shapes=[pltpu.VMEM((1, FOX_HB * qb), f32), pltpu.VMEM((1, FOX_HB * qb), f32),
                        pltpu.VMEM((DH, FOX_HB * qb), f32)],
        compiler_params=pltpu.CompilerParams(dimension_semantics=("parallel", "parallel", "arbitrary"),
                                             vmem_limit_bytes=VMEM_LIMIT),
        name="fox_prompt",
    )(q, k, vt, dq, dk)


ROWS = 64
SEL_COLS = 40


def _online_update(s, v, m_ref, l_ref, acc_ref):
    m_old = m_ref[...]
    m_new = jnp.maximum(m_old, jnp.max(s, axis=-1, keepdims=True))
    alpha = jnp.exp(m_old - m_new)
    p = jnp.exp(s - m_new)
    l_ref[...] = alpha * l_ref[...] + jnp.sum(p, axis=-1, keepdims=True)
    m_ref[...] = m_new
    acc_ref[...] = alpha * acc_ref[...] + jnp.dot(p.astype(CDT), v, preferred_element_type=f32)


def _sample_body(pt_ref, qa_ref, qb_ref, mrow_ref, selc_ref, seln_ref, bsel_ref, foxc_ref, foxn_ref, fb_ref,
                 win_ref, winn_ref, bwin_ref, osel_ref, ofox_ref, owin_ref,
                 ms_ref, ls_ref, as_ref, mf_ref, lf_ref, af_ref, *, past):
    pg = pl.program_id(1)
    n_pg = pl.num_programs(1)
    ps = PAGE_SIZE
    w_a = G_A * DH
    w_b = H_B * DH
    ts = ROWS // H_A
    i_row = lax.broadcasted_iota(jnp.int32, (ROWS, ps), 0) % ts
    j_col = lax.broadcasted_iota(jnp.int32, (ROWS, ps), 1)
    qa = qa_ref[0]
    qb = qb_ref[0]

    @pl.when(pg == 0)
    def _():
        ms_ref[...] = jnp.full(ms_ref.shape, NEG, f32)
        ls_ref[...] = jnp.zeros(ls_ref.shape, f32)
        as_ref[...] = jnp.zeros(as_ref.shape, f32)
        mf_ref[...] = jnp.full(mf_ref.shape, NEG, f32)
        lf_ref[...] = jnp.zeros(lf_ref.shape, f32)
        af_ref[...] = jnp.zeros(af_ref.shape, f32)
        kw = win_ref[0, 0, :, 0:w_a].astype(CDT)
        vw = win_ref[0, 0, :, w_a:2 * w_a].astype(CDT)
        s1 = _nt_dot(qa, kw) + bwin_ref[:, 0:WINDOW]
        iw = lax.broadcasted_iota(jnp.int32, (ROWS, WINDOW), 0) % ts
        jw = lax.broadcasted_iota(jnp.int32, (ROWS, WINDOW), 1)
        s1 = jnp.where(jw > iw, s1, NEG)
        kn = winn_ref[0, :, 0:w_a].astype(CDT)
        vn = winn_ref[0, :, w_a:2 * w_a].astype(CDT)
        s2 = _nt_dot(qa, kn) + bwin_ref[:, WINDOW:WINDOW + ps]
        s2 = jnp.where(j_col <= i_row, s2, NEG)
        m = jnp.maximum(jnp.max(s1, axis=-1, keepdims=True), jnp.max(s2, axis=-1, keepdims=True))
        p1 = jnp.exp(s1 - m)
        p2 = jnp.exp(s2 - m)
        l = jnp.sum(p1, axis=-1, keepdims=True) + jnp.sum(p2, axis=-1, keepdims=True)
        o = (jnp.dot(p1.astype(CDT), vw, preferred_element_type=f32)
             + jnp.dot(p2.astype(CDT), vn, preferred_element_type=f32))
        owin_ref[0] = o / l

    causal = (pg * ps + j_col) <= (past + i_row)

    def sel_step(blk_ref):
        k = blk_ref[0, :, 0:w_a].astype(CDT)
        v = blk_ref[0, :, w_a:2 * w_a].astype(CDT)
        s = _nt_dot(qa, k) + bsel_ref[0]
        blk_of_key = 2 * pg + lax.broadcasted_iota(jnp.int32, (SEL_COLS, ps), 1) // L_SEL
        expand = jnp.where(lax.broadcasted_iota(jnp.int32, (SEL_COLS, ps), 0) == blk_of_key, 1.0, 0.0).astype(CDT)
        picked = jnp.dot(mrow_ref[0].astype(CDT), expand, preferred_element_type=f32) > 0.5
        s = jnp.where(picked & causal, s, NEG)
        _online_update(s, v, ms_ref, ls_ref, as_ref)

    def fox_step(blk_ref):
        k = blk_ref[0, :, 0:w_b].astype(CDT)
        v = blk_ref[0, :, w_b:2 * w_b].astype(CDT)
        s = _nt_dot(qb, k) + fb_ref[0, 0]
        s = jnp.where(causal, s, NEG)
        _online_update(s, v, mf_ref, lf_ref, af_ref)

    @pl.when(pg < n_pg - 1)
    def _():
        sel_step(selc_ref)
        fox_step(foxc_ref)

    @pl.when(pg == n_pg - 1)
    def _():
        sel_step(seln_ref)
        fox_step(foxn_ref)
        osel_ref[0] = as_ref[...] / ls_ref[...]
        ofox_ref[0] = af_ref[...] / lf_ref[...]


def sample_attention(pt_l, qa_bd, qb_bd, mrows, sel_cache2, sel_new, bias_sel, fox_cache2, fox_new, fbias,
                     win_state, win_new, bias_win, *, layer, past):
    bs, n_pages = pt_l.shape
    ps = PAGE_SIZE
    n_pg = n_pages + 1
    w_a = G_A * DH
    w_b = H_B * DH
    last = n_pages - 1
    grid_spec = pltpu.PrefetchScalarGridSpec(
        num_scalar_prefetch=1,
        grid=(bs, n_pg),
        in_specs=[
            pl.BlockSpec((1, ROWS, w_a), lambda b, p, pt: (b, 0, 0)),
            pl.BlockSpec((1, ROWS, w_b), lambda b, p, pt: (b, 0, 0)),
            pl.BlockSpec((1, ROWS, SEL_COLS), lambda b, p, pt: (b, 0, 0)),
            pl.BlockSpec((1, ps, 2 * w_a), lambda b, p, pt: (pt[b, jnp.minimum(p, last)], 0, 0)),
            pl.BlockSpec((1, ps, 2 * w_a), lambda b, p, pt: (b, 0, 0)),
            pl.BlockSpec((1, ROWS, ps), lambda b, p, pt: (p, 0, 0)),
            pl.BlockSpec((1, ps, 2 * w_b), lambda b, p, pt: (pt[b, jnp.minimum(p, last)], 0, 0)),
            pl.BlockSpec((1, ps, 2 * w_b), lambda b, p, pt: (b, 0, 0)),
            pl.BlockSpec((1, 1, ROWS, ps), lambda b, p, pt: (b, p, 0, 0)),
            pl.BlockSpec((1, 1, WINDOW, 2 * w_a), lambda b, p, pt: (layer, b, 0, 0)),
            pl.BlockSpec((1, ps, 2 * w_a), lambda b, p, pt: (b, 0, 0)),
            pl.BlockSpec((ROWS, WINDOW + ps), lambda b, p, pt: (0, 0)),
        ],
        out_specs=[pl.BlockSpec((1, ROWS, w_a), lambda b, p, pt: (b, 0, 0)),
                   pl.BlockSpec((1, ROWS, w_b), lambda b, p, pt: (b, 0, 0)),
                   pl.BlockSpec((1, ROWS, w_a), lambda b, p, pt: (b, 0, 0))],
        scratch_shapes=[pltpu.VMEM((ROWS, 1), f32), pltpu.VMEM((ROWS, 1), f32), pltpu.VMEM((ROWS, w_a), f32),
                        pltpu.VMEM((ROWS, 1), f32), pltpu.VMEM((ROWS, 1), f32), pltpu.VMEM((ROWS, w_b), f32)],
    )
    return pl.pallas_call(
        functools.partial(_sample_body, past=past),
        grid_spec=grid_spec,
        out_shape=[jax.ShapeDtypeStruct((bs, ROWS, w_a), f32), jax.ShapeDtypeStruct((bs, ROWS, w_b), f32),
                   jax.ShapeDtypeStruct((bs, ROWS, w_a), f32)],
        compiler_params=pltpu.CompilerParams(dimension_semantics=("parallel", "arbitrary"),
                                             vmem_limit_bytes=VMEM_LIMIT),
        name="sample_attention",
    )(pt_l, qa_bd, qb_bd, mrows, sel_cache2, sel_new, bias_sel, fox_cache2, fox_new, fbias, win_state, win_new,
      bias_win)


def sample_bias_tables(table, past, ts, n_pg):
    tg = table.reshape(N_BUCKETS, H_A)
    tq = past + np.arange(ts)
    s_abs = np.arange(n_pg * PAGE_SIZE)
    bsel = t5_bucket_np(tq[:, None] - s_abs[None, :])
    k_pos_w = np.concatenate([past - WINDOW + np.arange(WINDOW), past + np.arange(PAGE_SIZE)])
    bwin = t5_bucket_np(tq[:, None] - k_pos_w[None, :])

    def rows(bk):
        return jnp.transpose(tg[bk], (2, 0, 1)).reshape(H_A * ts, bk.shape[1])

    bias_sel = rows(bsel).reshape(ROWS, n_pg, PAGE_SIZE).transpose(1, 0, 2)
    return bias_sel.astype(f32), rows(bwin).astype(f32)


def block_diag_rows(q, heads_per_group):
    bs, ts, h, dh = q.shape
    n_c = h // heads_per_group
    qh = q.transpose(0, 2, 1, 3)
    onehot = (np.arange(h)[:, None] // heads_per_group == np.arange(n_c)[None, :]).astype(np.float32)
    out = qh[:, :, :, None, :] * onehot[None, :, None, :, None]
    return out.reshape(bs, h * ts, n_c * dh)


def diag_extract(o, heads_per_group, ts):
    bs = o.shape[0]
    h = o.shape[1] // ts
    n_c = h // heads_per_group
    own = (np.arange(h)[:, None] // heads_per_group == np.arange(n_c)[None, :]).astype(np.float32)
    o5 = o.reshape(bs, h, ts, n_c, DH) * own[None, :, None, :, None]
    return o5.sum(axis=3).transpose(0, 2, 1, 3)


def sample_cmp_select(q_a, q_pos, k_cmp, v_cmp, table):
    b, nb_c = k_cmp.shape[:2]
    nb_s = nb_c * L_CMP // L_SEL
    n_sel = min(N_SEL, nb_s)
    end_pos = jnp.arange(nb_c) * L_CMP + (L_CMP - 1)
    tb = q_pos.shape[0]
    qg = q_a.reshape(b, tb, G_A, HG_A, DH)
    dist_c = q_pos[:, None] - end_pos[None, :]
    mask_c = dist_c >= 0
    s = jnp.einsum('btgrd,bngd->bgrtn', qg, k_cmp).astype(f32) * SCALE + rel_bias_grid(dist_c, table)
    p = jnp.where(mask_c, jax.nn.softmax(jnp.where(mask_c, s, NEG), axis=-1), 0.0)
    o_cmp = jnp.einsum('bgrtn,bngd->btgrd', p.astype(v_cmp.dtype), v_cmp)
    imp = p.sum(2).reshape(b, G_A, tb, nb_s, L_SEL // L_CMP).sum(-1)
    cur = (q_pos // L_SEL)[:, None]
    bidx = jnp.arange(nb_s)[None, :]
    imp = jnp.where((bidx == 0) | (bidx == cur), FORCE, imp)
    imp = jnp.where(bidx <= cur, imp, NEG)
    _, sel = lax.top_k(imp, n_sel)
    return o_cmp.reshape(b, tb, H_A, DH), sel


def sample_group_attention(l, q_a, kv_s, kv_w, q_b, k_b, v_b, logf, sel, cache_sel, cache_fox, cache_logf,
                           win_state, page_table, table):
    bs, ts = q_a.shape[:2]
    n_pages = page_table.shape[1]
    past = n_pages * PAGE_SIZE
    n_pool = cache_sel.shape[1]
    ps = PAGE_SIZE
    pt_l = (page_table + l * n_pool).astype(jnp.int32)
    qa_bd = block_diag_rows(q_a * SCALE, HG_A).astype(CDT)
    qb_bd = block_diag_rows(q_b * SCALE, 1).astype(CDT)
    onehot = jnp.sum(jax.nn.one_hot(sel, SEL_COLS, dtype=f32), axis=-2)
    mrows = jnp.broadcast_to(onehot[:, :, None], (bs, G_A, HG_A, ts, SEL_COLS)).reshape(bs, ROWS, SEL_COLS)
    padn = lambda a: jnp.pad(a.reshape(bs, ts, -1), ((0, 0), (0, ps - ts), (0, 0)))
    sel_new = padn(kv_s)
    win_new = padn(kv_w)
    fox_new = padn(jnp.stack([k_b, v_b], axis=2))
    bias_sel, bias_win = sample_bias_tables(table, past, ts, n_pages + 1)
    lf_past = cache_logf.reshape(cache_logf.shape[0], n_pool, ps * H_B)[l][page_table].reshape(bs, past, H_B)
    dk = jnp.cumsum(jnp.concatenate([lf_past, logf], axis=1), axis=1)
    dk_pad = jnp.pad(dk, ((0, 0), (0, ps - ts), (0, 0)))
    dq = dk[:, past:]
    fbias = dq.transpose(0, 2, 1)[:, :, :, None] - dk_pad.transpose(0, 2, 1)[:, :, None, :]
    fbias = fbias.reshape(bs, ROWS, n_pages + 1, ps).transpose(0, 2, 1, 3)
    sel_cache2 = cache_sel.reshape(-1, ps, 2 * G_A * DH)
    fox_cache2 = cache_fox.reshape(-1, ps, 2 * H_B * DH)
    win2 = win_state.reshape(win_state.shape[0], bs, WINDOW, 2 * G_A * DH)
    o_sel, o_fox, o_win = sample_attention(pt_l, qa_bd, qb_bd, mrows, sel_cache2, sel_new, bias_sel, fox_cache2,
                                           fox_new, fbias, win2, win_new, bias_win, layer=l, past=past)
    return diag_extract(o_sel, HG_A, ts), diag_extract(o_win, HG_A, ts), diag_extract(o_fox, 1, ts)


def _sigmoid(x):
    return 1.0 / (1.0 + jnp.exp(-x))

def _mix_body(x_ref, oc_ref, os_ref, ow_ref, gn_ref, ob_ref, ga_ref, gb_ref, wpa_ref, wpb_ref, wo_ref, o_ref):
    gn = _sigmoid(gn_ref[...])
    o_a = gn[:, 0:W_A] * oc_ref[...] + gn[:, W_A:2 * W_A] * os_ref[...] + gn[:, 2 * W_A:3 * W_A] * ow_ref[...]
    pa = jnp.dot(o_a.astype(CDT), wpa_ref[...], preferred_element_type=f32)
    pb = jnp.dot(ob_ref[...].astype(CDT), wpb_ref[...], preferred_element_type=f32)
    mix = _sigmoid(ga_ref[...]) * pa + _sigmoid(gb_ref[...]) * pb
    o_ref[...] = x_ref[...] + jnp.dot(mix.astype(CDT), wo_ref[...], preferred_element_type=f32)

def mix_residual(x2d, o_cmp, o_sel, o_win, gn_exp, o_b, gate_a, gate_b, w_pa, w_pb, w_o, block_n=256):
    n, d = x2d.shape
    row = lambda w: pl.BlockSpec((block_n, w), lambda i: (i, 0))
    full = lambda a: pl.BlockSpec(a.shape, lambda i: (0, 0))
    return pl.pallas_call(
        _mix_body,
        grid=(n // block_n,),
        in_specs=[row(d), row(W_A), row(W_A), row(W_A), row(3 * W_A), row(W_B), row(d), row(d),
                  full(w_pa), full(w_pb), full(w_o)],
        out_specs=row(d),
        out_shape=jax.ShapeDtypeStruct((n, d), f32),
        compiler_params=pltpu.CompilerParams(dimension_semantics=("parallel",), vmem_limit_bytes=VMEM_LIMIT),
        name="mix_residual",
    )(x2d, o_cmp, o_sel, o_win, gn_exp, o_b, gate_a, gate_b, w_pa, w_pb, w_o)

def _ple_body(x_ref, p_ref, g_ref, wpe_ref, wpg_ref, gf_ref, o_ref, y_ref):
    x = x_ref[...]
    ms = jnp.mean(x * x, axis=-1, keepdims=True)
    h = x * lax.rsqrt(ms + EPS) * g_ref[...]
    gate = _sigmoid(jnp.dot(h.astype(CDT), wpg_ref[...], preferred_element_type=f32))
    xo = x + jnp.dot(p_ref[...].astype(CDT), wpe_ref[...], preferred_element_type=f32) * gate
    o_ref[...] = xo
    ms2 = jnp.mean(xo * xo, axis=-1, keepdims=True)
    y_ref[...] = xo * lax.rsqrt(ms2 + EPS) * gf_ref[...]

def ple_residual(x2d, p2d, g_p, w_pe, w_pg, g_final, block_n=256):
    n, d = x2d.shape
    row = lambda w: pl.BlockSpec((block_n, w), lambda i: (i, 0))
    full = lambda a: pl.BlockSpec(a.shape, lambda i: (0, 0))
    out = jax.ShapeDtypeStruct((n, d), f32)
    return pl.pallas_call(
        _ple_body,
        grid=(n // block_n,),
        in_specs=[row(d), row(p2d.shape[1]), pl.BlockSpec((1, d), lambda i: (0, 0)), full(w_pe), full(w_pg),
                  pl.BlockSpec((1, d), lambda i: (0, 0))],
        out_specs=[row(d), row(d)],
        out_shape=[out, out],
        compiler_params=pltpu.CompilerParams(dimension_semantics=("parallel",), vmem_limit_bytes=VMEM_LIMIT),
        name="ple_residual",
    )(x2d, p2d, g_p.reshape(1, d), w_pe, w_pg, g_final.reshape(1, d))


def rms_norm(x, g):
    x32 = x.astype(jnp.float32)
    y = x32 * lax.rsqrt(jnp.mean(x32 * x32, axis=-1, keepdims=True) + EPS)
    return (y * g.astype(jnp.float32)).astype(x.dtype)


def t5_bucket(dist):
    n = jnp.maximum(dist, 0)
    max_exact = N_BUCKETS // 2
    nf = jnp.maximum(n, 1).astype(jnp.float32)
    large = max_exact + (jnp.log(nf / max_exact) / math.log(MAX_DISTANCE / max_exact)
                         * (N_BUCKETS - max_exact)).astype(jnp.int32)
    large = jnp.minimum(large, N_BUCKETS - 1)
    return jnp.where(n < max_exact, n, large)


def rel_bias_grid(dist, table):
    b = table[t5_bucket(dist)].astype(jnp.float32)
    return jnp.moveaxis(b.reshape(*dist.shape, G_A, HG_A), (-2, -1), (0, 1))


def over_query_blocks(fn, q_pos, *qs):
    tq = q_pos.shape[0]
    if tq <= Q_BLOCK or tq % Q_BLOCK:
        return fn(q_pos, *qs)
    nb = tq // Q_BLOCK
    split = lambda a: jnp.moveaxis(a.reshape(a.shape[0], nb, Q_BLOCK, *a.shape[2:]), 1, 0)
    out = lax.map(lambda args: fn(*args), (q_pos.reshape(nb, Q_BLOCK),) + tuple(split(a) for a in qs))
    merge = lambda a: jnp.moveaxis(a, 0, 1).reshape(a.shape[1], tq, *a.shape[3:])
    return jax.tree_util.tree_map(merge, out)


def masked_attend(q, k, v, bias, mask):
    s = jnp.einsum('btgrd,bsgd->bgrts', q, k).astype(jnp.float32) * SCALE + bias
    p = jnp.where(mask, jax.nn.softmax(jnp.where(mask, s, NEG), axis=-1), 0.0)
    return jnp.einsum('bgrts,bsgd->btgrd', p.astype(v.dtype), v)


def window_attend(q, q_pos, k, v, k_pos, table):
    b, tq = q.shape[:2]
    dist = q_pos[:, None] - k_pos[None, :]
    mask = (dist >= 0) & (dist < WINDOW) & (k_pos[None, :] >= 0)
    o = masked_attend(q.reshape(b, tq, G_A, HG_A, DH), k, v, rel_bias_grid(dist, table), mask)
    return o.reshape(b, tq, H_A, DH)


def window_prompt(q, kw, vw, table):
    t = q.shape[1]
    padf = lambda a: jnp.pad(a, ((0, 0), (WINDOW, 0), (0, 0), (0, 0)))
    kp, vp = padf(kw), padf(vw)

    def blk(q_pos, qb):
        span = WINDOW + q_pos.shape[0]
        start = q_pos[0]
        kb = lax.dynamic_slice_in_dim(kp, start, span, axis=1)
        vb = lax.dynamic_slice_in_dim(vp, start, span, axis=1)
        k_pos = start - WINDOW + jnp.arange(span)
        return window_attend(qb, q_pos, kb, vb, k_pos, table)

    return over_query_blocks(blk, jnp.arange(t), q)


def nsa_cmp_sel(q, q_pos, kc, vc, ks, vs, w_ck, w_cv, pe, table):
    b, tk = kc.shape[:2]
    pad = (-tk) % L_SEL
    padt = lambda a: jnp.pad(a, ((0, 0), (0, pad), (0, 0), (0, 0)))
    kc, vc, ks, vs = padt(kc), padt(vc), padt(ks), padt(vs)
    tp = tk + pad
    nb_c, nb_s = tp // L_CMP, tp // L_SEL
    n_sel = min(N_SEL, nb_s)
    blocks = lambda a: a.reshape(b, nb_c, L_CMP, G_A, DH) + pe[:, None, :].astype(a.dtype)
    k_cmp = jnp.einsum('bnlgd,lde->bnge', blocks(kc), w_ck)
    v_cmp = jnp.einsum('bnlgd,lde->bnge', blocks(vc), w_cv)
    end_pos = jnp.arange(nb_c) * L_CMP + (L_CMP - 1)
    ks_b = jnp.moveaxis(ks.reshape(b, nb_s, L_SEL, G_A, DH), 3, 1)
    vs_b = jnp.moveaxis(vs.reshape(b, nb_s, L_SEL, G_A, DH), 3, 1)
    b_ix = jnp.arange(b)[:, None, None, None]
    g_ix = jnp.arange(G_A)[None, :, None, None]
    table_g = table.reshape(N_BUCKETS, G_A, HG_A)

    def blk(q_pos, qb):
        tb = q_pos.shape[0]
        qg = qb.reshape(b, tb, G_A, HG_A, DH)
        dist_c = q_pos[:, None] - end_pos[None, :]
        mask_c = dist_c >= 0
        s = jnp.einsum('btgrd,bngd->bgrtn', qg, k_cmp).astype(jnp.float32) * SCALE + rel_bias_grid(dist_c, table)
        p = jnp.where(mask_c, jax.nn.softmax(jnp.where(mask_c, s, NEG), axis=-1), 0.0)
        o_cmp = jnp.einsum('bgrtn,bngd->btgrd', p.astype(v_cmp.dtype), v_cmp)
        imp = p.sum(2).reshape(b, G_A, tb, nb_s, L_SEL // L_CMP).sum(-1)
        cur = (q_pos // L_SEL)[:, None]
        bidx = jnp.arange(nb_s)[None, :]
        imp = jnp.where((bidx == 0) | (bidx == cur), FORCE, imp)
        imp = jnp.where(bidx <= cur, imp, NEG)
        _, sel = lax.top_k(imp, n_sel)
        kg = ks_b[b_ix, g_ix, sel]
        vg = vs_b[b_ix, g_ix, sel]
        tok = sel[..., None] * L_SEL + jnp.arange(L_SEL)
        dist_s = q_pos[None, None, :, None, None] - tok
        bias_s = jnp.moveaxis(table_g[t5_bucket(dist_s), g_ix[..., None]], -1, 2).astype(jnp.float32)
        mask_s = (dist_s >= 0)[:, :, None]
        s2 = jnp.einsum('btgrd,bgtnld->bgrtnl', qg, kg).astype(jnp.float32) * SCALE + bias_s
        s2 = jnp.where(mask_s, s2, NEG).reshape(b, G_A, HG_A, tb, n_sel * L_SEL)
        p2 = jax.nn.softmax(s2, axis=-1)
        o_sel = jnp.einsum('bgrtk,bgtkd->btgrd', p2.astype(vg.dtype), vg.reshape(b, G_A, tb, n_sel * L_SEL, DH))
        return (o_cmp.reshape(b, tb, H_A, DH), o_sel.reshape(b, tb, H_A, DH))

    return over_query_blocks(blk, q_pos, q)


def fox_attend(q, dq, q_pos, k, v, dk, k_pos):
    b = q.shape[0]
    dk_t = jnp.moveaxis(dk, 2, 1)[:, :, None, None, :]

    def blk(q_pos, qb, dqb):
        tb = q_pos.shape[0]
        bias = jnp.moveaxis(dqb, 2, 1)[:, :, None, :, None] - dk_t
        mask = k_pos[None, :] <= q_pos[:, None]
        o = masked_attend(qb.reshape(b, tb, H_B, 1, DH), k, v, bias, mask)
        return o.reshape(b, tb, H_B, DH)

    return over_query_blocks(blk, q_pos, q, dq)


def project(x, g, w_in_pad):
    b, t = x.shape[:2]
    z = norm_proj(x.reshape(b * t, D_MODEL), g, w_in_pad)[:, :N_IN].reshape(b, t, N_IN)
    (q_a, kv_c, kv_s, kv_w, g_nsa, q_b, k_b, v_b, f_b, gate_a, gate_b) = jnp.split(
        z, np.cumsum(SPLIT_SIZES)[:-1].tolist(), axis=-1)
    kv = lambda a: a.reshape(b, t, 2, G_A, DH)
    hd = lambda a, nh: a.reshape(b, t, nh, DH)
    return (hd(q_a, H_A), kv(kv_c), kv(kv_s), kv(kv_w), g_nsa.reshape(b, t, 3, H_A),
            hd(q_b, H_B), hd(k_b, H_B), hd(v_b, H_B), f_b, gate_a, gate_b)


def layer_tail(x, o_cmp, o_sel, o_win, g_nsa, o_b, gate_a, gate_b, p_l, lw, peer_w, g_final):
    (w_pa, w_pb, w_o, g_f, g_p, w_pe, w_pg) = lw
    b, t, d = x.shape
    n = b * t
    gn_exp = jnp.repeat(g_nsa.reshape(n, 3 * H_A), DH, axis=1)
    x1 = mix_residual(x.reshape(n, d), o_cmp.reshape(n, W_A), o_sel.reshape(n, W_A), o_win.reshape(n, W_A), gn_exp,
                      o_b.reshape(n, W_B), gate_a.reshape(n, d), gate_b.reshape(n, d), w_pa, w_pb, w_o)
    x2 = peer_residual(x1, g_f, peer_w)
    x3, y = ple_residual(x2, p_l.reshape(n, -1), g_p, w_pe, w_pg, g_final)
    return x3.reshape(b, t, d), y.reshape(b, t, d)


def kernel(x_prompt, x_sample, cache_nsa_cmp_kv, cache_nsa_sel_kv, cache_fox_kv, cache_fox_logf,
           state_nsa_win_kv, page_table, p_prompt, p_sample, g_mix, w_in, b_forget, w_cmp_k, w_cmp_v,
           pe_cmp, rel_bias_table, w_proj_a, w_proj_b, w_out, g_ffn, peer_w_q, peer_sub_keys,
           peer_u, peer_v, g_ple, w_ple, w_ple_gate, g_final):
    depth = w_in.shape[0]
    bp, tp = x_prompt.shape[:2]
    bs, ts = x_sample.shape[:2]
    past = page_table.shape[1] * PAGE_SIZE
    win_p = min(WINDOW, tp)
    win_buf = state_nsa_win_kv.shape[2]
    pos_p = jnp.arange(tp)
    pos_s = past + jnp.arange(ts)
    pos_all = jnp.arange(past + ts)
    pos_w = past - win_buf + jnp.arange(win_buf + ts)
    n_in_pad = -(-N_IN // LANES) * LANES
    w_in_pad = jnp.pad(w_in, ((0, 0), (0, 0), (0, n_in_pad - N_IN))).astype(CDT)
    bias_c, bias_s = nsa_bias_tables(rel_bias_table, tp)
    xp, xs = x_prompt, x_sample
    pc, psel, pw, pfk, pfl = [], [], [], [], []
    sc, ssel, sw, sfk, sfl = [], [], [], [], []
    for l in range(depth):
        lw = (w_proj_a[l].astype(CDT), w_proj_b[l].astype(CDT), w_out[l].astype(CDT), g_ffn[l], g_ple[l],
              w_ple[l].astype(CDT), w_ple_gate[l].astype(CDT))
        peer_w = (peer_w_q[l].T.astype(CDT),
                  peer_sub_keys[l].reshape(PEER_HEADS * 2, N_KEYS, D_HALF).astype(CDT),
                  peer_u[l].astype(CDT), peer_v[l].T.astype(CDT))
        cmp_w = (w_cmp_k[l], w_cmp_v[l], pe_cmp[l], rel_bias_table)

        (q_a, kv_c, kv_s, kv_w, g_nsa, q_b, k_b, v_b, f_b, gate_a, gate_b) = project(xp, g_mix[l], w_in_pad[l])
        k_cmp, v_cmp = nsa_compress(kv_c, w_cmp_k[l], w_cmp_v[l], pe_cmp[l])
        o_cmp, o_sel, o_win = nsa_prompt(q_a.reshape(bp, tp, W_A), k_cmp, v_cmp, kv_s, kv_w, bias_c, bias_s)
        o_cmp = o_cmp.reshape(bp, tp, H_A, DH)
        o_sel = o_sel.reshape(bp, tp, H_A, DH)
        o_win = o_win.reshape(bp, tp, H_A, DH)
        logf = jax.nn.log_sigmoid(f_b.astype(f32) + b_forget[l].astype(f32))
        dcum = jnp.cumsum(logf, axis=1)
        o_b = fox_prompt(q_b, k_b, v_b, dcum).reshape(bp, tp, H_B, DH)
        xp, y_prompt = layer_tail(xp, o_cmp, o_sel, o_win, g_nsa, o_b, gate_a, gate_b, p_prompt[l], lw, peer_w, g_final)
        pc.append(kv_c)
        psel.append(kv_s)
        pw.append(kv_w[:, tp - win_p:])
        pfk.append(jnp.stack([k_b, v_b], axis=2))
        pfl.append(logf)

        (q_a, kv_c, kv_s, kv_w, g_nsa, q_b, k_b, v_b, f_b, gate_a, gate_b) = project(xs, g_mix[l], w_in_pad[l])
        kc_pool, vc_pool = nsa_compress(cache_nsa_cmp_kv[l], *cmp_w[:3])
        kc_new, vc_new = nsa_compress(jnp.pad(kv_c, ((0, 0), (0, L_SEL - ts), (0, 0), (0, 0), (0, 0))), *cmp_w[:3])
        gather_pages = lambda a: a[page_table].reshape(bs, past // L_CMP, G_A, DH)
        k_cmp = jnp.concatenate([gather_pages(kc_pool), kc_new], axis=1)
        v_cmp = jnp.concatenate([gather_pages(vc_pool), vc_new], axis=1)
        o_cmp, sel = sample_cmp_select(q_a, pos_s, k_cmp, v_cmp, rel_bias_table)
        w_all = jnp.concatenate([state_nsa_win_kv[l], kv_w], axis=1)
        logf = jax.nn.log_sigmoid(f_b.astype(f32) + b_forget[l].astype(f32))
        o_sel, o_win, o_b = sample_group_attention(l, q_a, kv_s, kv_w, q_b, k_b, v_b, logf, sel, cache_nsa_sel_kv,
                                                   cache_fox_kv, cache_fox_logf, state_nsa_win_kv, page_table,
                                                   rel_bias_table)
        xs, y_sample = layer_tail(xs, o_cmp, o_sel, o_win, g_nsa, o_b, gate_a, gate_b, p_sample[l], lw, peer_w, g_final)
        sc.append(kv_c)
        ssel.append(kv_s)
        sw.append(w_all[:, ts:])
        sfk.append(jnp.stack([k_b, v_b], axis=2))
        sfl.append(logf)

    return (y_prompt, y_sample,
            jnp.stack(pc), jnp.stack(psel), jnp.stack(pw), jnp.stack(pfk), jnp.stack(pfl),
            jnp.stack(sc), jnp.stack(ssel), jnp.stack(sw), jnp.stack(sfk), jnp.stack(sfl))
```

```python
import functools
import math

import jax
import jax.numpy as jnp
import numpy as np
from jax import lax
from jax.experimental import pallas as pl
from jax.experimental.pallas import tpu as pltpu

D_MODEL = 1024
PAGE_SIZE = 128
H_A = 8
G_A = 2
HG_A = H_A // G_A
DH = 64
L_CMP = 32
L_SEL = 64
N_SEL = 16
WINDOW = 512
H_B = 8
N_BUCKETS = 32
MAX_DISTANCE = 128
PEER_HEADS = 8
N_KEYS = 128
N_EXPERTS = N_KEYS * N_KEYS
PEER_TOPK = 16
D_KEY = 256
D_HALF = D_KEY // 2
PEER_TOKEN_BLOCK = 128
Q_BLOCK = 128
EPS = 1e-6
NEG = -1e30
FORCE = 1e4
SCALE = DH ** -0.5
W_A = H_A * DH
W_B = H_B * DH
SPLIT_SIZES = (W_A, 2 * G_A * DH, 2 * G_A * DH, 2 * G_A * DH, 3 * H_A,
               W_B, W_B, W_B, H_B, D_MODEL, D_MODEL)
N_IN = sum(SPLIT_SIZES)

LANES = 128
VMEM_LIMIT = 48 * 1024 * 1024
CDT = jnp.bfloat16
f32 = jnp.float32


def _nt_dot(a, b):
    return lax.dot_general(a, b, (((1,), (1,)), ((), ())), preferred_element_type=f32)


def _norm_proj_body(x_ref, g_ref, w_ref, o_ref):
    x = x_ref[...]
    ms = jnp.mean(x * x, axis=-1, keepdims=True)
    h = x * lax.rsqrt(ms + EPS) * g_ref[...]
    o_ref[...] = jnp.dot(h.astype(CDT), w_ref[...], preferred_element_type=f32)


def norm_proj(x2d, g, w, block_n=256):
    n, d = x2d.shape
    m = w.shape[1]
    return pl.pallas_call(
        _norm_proj_body,
        grid=(n // block_n,),
        in_specs=[pl.BlockSpec((block_n, d), lambda i: (i, 0)),
                  pl.BlockSpec((1, d), lambda i: (0, 0)),
                  pl.BlockSpec((d, m), lambda i: (0, 0))],
        out_specs=pl.BlockSpec((block_n, m), lambda i: (i, 0)),
        out_shape=jax.ShapeDtypeStruct((n, m), f32),
        compiler_params=pltpu.CompilerParams(dimension_semantics=("parallel",),
                                             vmem_limit_bytes=VMEM_LIMIT),
        name="norm_proj",
    )(x2d, g.reshape(1, d), w)


def _compress_body(a_ref, pe_ref, w_ref, o_ref):
    o_ref[...] = jnp.dot((a_ref[...] + pe_ref[...]).astype(CDT), w_ref[...], preferred_element_type=f32)


def nsa_compress(kv_c, w_ck, w_cv, pe, block_m=128):
    b, t = kv_c.shape[:2]
    nb_c = t // L_CMP
    row = L_CMP * 2 * G_A * DH
    eye = jnp.eye(2, dtype=f32)
    w_big = jnp.einsum('klde,kK,gG->lkgdKGe', jnp.stack([w_ck, w_cv]), eye, eye).reshape(row, 2 * G_A * DH).astype(CDT)
    pe_big = jnp.broadcast_to(pe[:, None, None, :], (L_CMP, 2, G_A, DH)).reshape(1, row)
    a = kv_c.reshape(b * nb_c, row)
    m = a.shape[0]
    out = pl.pallas_call(
        _compress_body,
        grid=(m // block_m,),
        in_specs=[pl.BlockSpec((block_m, row), lambda i: (i, 0)),
                  pl.BlockSpec((1, row), lambda i: (0, 0)),
                  pl.BlockSpec((row, 2 * G_A * DH), lambda i: (0, 0))],
        out_specs=pl.BlockSpec((block_m, 2 * G_A * DH), lambda i: (i, 0)),
        out_shape=jax.ShapeDtypeStruct((m, 2 * G_A * DH), f32),
        compiler_params=pltpu.CompilerParams(dimension_semantics=("parallel",), vmem_limit_bytes=VMEM_LIMIT),
        name="nsa_compress",
    )(a, pe_big, w_big)
    out = out.reshape(b, nb_c, 2, G_A, DH)
    return out[:, :, 0], out[:, :, 1]


def t5_bucket_np(dist):
    n = np.maximum(dist, 0)
    nf = np.maximum(n, 1).astype(np.float32)
    max_exact = N_BUCKETS // 2
    large = max_exact + (np.log(nf / np.float32(max_exact)) / np.float32(math.log(MAX_DISTANCE / max_exact))
                         * np.float32(N_BUCKETS - max_exact)).astype(np.int32)
    large = np.minimum(large, N_BUCKETS - 1)
    return np.where(n < max_exact, n, large).astype(np.int32)


def cmp_perm(nb_c):
    half = nb_c // 2
    npr = np.arange(nb_c)
    return 2 * (npr % half) + npr // half


def nsa_bias_tables(table, t):
    nb_c = t // L_CMP
    nq = t // Q_BLOCK
    tg = table.reshape(N_BUCKETS, G_A, HG_A)
    ratio = Q_BLOCK // L_CMP
    u_min = -(nb_c - 1)
    u = np.arange(u_min, ratio * (nq - 1) + 1)
    d2 = L_CMP * u[:, None] + np.arange(Q_BLOCK)[None, :] - (L_CMP - 1)
    rows = tg[t5_bucket_np(d2)]
    row_of = ratio * np.arange(nq)[:, None] - cmp_perm(nb_c)[None, :] - u_min
    bias_c = jnp.transpose(rows[row_of], (3, 4, 0, 1, 2))
    i = np.arange(Q_BLOCK)
    d0 = i[None, :] - i[:, None]
    bs = np.stack([t5_bucket_np(d0), t5_bucket_np(d0 + Q_BLOCK),
                   np.full((Q_BLOCK, Q_BLOCK), N_BUCKETS - 1, np.int32)])
    bias_s = jnp.transpose(tg[bs], (3, 0, 1, 4, 2)).reshape(G_A, 3, Q_BLOCK, HG_A * Q_BLOCK)
    return bias_c.astype(f32), bias_s.astype(f32)


def _softmax_cols(s, mask):
    s = jnp.where(mask, s, NEG)
    m = jnp.max(s, axis=0, keepdims=True)
    e = jnp.where(mask, jnp.exp(s - m), 0.0)
    l = jnp.sum(e, axis=0, keepdims=True)
    return e / jnp.where(l > 0.0, l, 1.0)


def _nsa_body(q_ref, kc_ref, vc_ref, bc_ref, ks_ref, vst_ref, kw_ref, vwt_ref, bs_ref, ocmp_ref, osel_ref, owin_ref,
              m_scr, l_scr, acc_scr, *, nb_c):
    qb = Q_BLOCK
    qi = pl.program_id(2)
    nb_s = nb_c // 2
    tpos = qi * qb + lax.broadcasted_iota(jnp.int32, (1, qb), 1)
    npr = lax.broadcasted_iota(jnp.int32, (nb_c, 1), 0)
    n_of = 2 * (npr % nb_s) + npr // nb_s
    mask_c = tpos >= (L_CMP * n_of + (L_CMP - 1))
    psum = jnp.zeros((nb_c, qb), f32)
    for r in range(HG_A):
        s_t = _nt_dot(kc_ref[0, 0], q_ref[0, 0, r]) + bc_ref[0, r, 0]
        p_t = _softmax_cols(s_t, mask_c)
        psum = psum + p_t
        o_t = jnp.dot(vc_ref[0, 0], p_t.astype(CDT), preferred_element_type=f32)
        ocmp_ref[0, :, r * DH:(r + 1) * DH] = o_t.T
    imp = psum[:nb_s] + psum[nb_s:]
    m_io = lax.broadcasted_iota(jnp.int32, (nb_s, qb), 0)
    cur = tpos // L_SEL
    imp = jnp.where((m_io == 0) | (m_io == cur), FORCE, imp)
    imp = jnp.where(m_io <= cur, imp, NEG)
    m_f = m_io.astype(f32)
    sel = jnp.zeros((nb_s, qb), f32)
    for _ in range(N_SEL):
        mx = jnp.max(imp, axis=0, keepdims=True)
        idx = jnp.min(jnp.where(imp == mx, m_f, float(nb_s)), axis=0, keepdims=True)
        hit = m_f == idx
        sel = jnp.where(hit, 1.0, sel)
        imp = jnp.where(hit, -jnp.inf, imp)
    sel_c = jnp.concatenate([sel.astype(CDT)] * HG_A, axis=1)
    q4 = q_ref[0, 0].reshape(HG_A * qb, DH)
    key_half = lax.broadcasted_iota(jnp.int32, (qb, nb_s), 0) // L_SEL
    m_col = lax.broadcasted_iota(jnp.int32, (qb, nb_s), 1)
    j_io = lax.broadcasted_iota(jnp.int32, (qb, HG_A * qb), 0)
    i_io = lax.broadcasted_iota(jnp.int32, (qb, HG_A * qb), 1) % qb

    def step(kb, bias_idx, k_ref, vt_ref, valid):
        k0 = pl.multiple_of(kb * qb, qb)
        s = _nt_dot(k_ref[0, 0, pl.ds(k0, qb), :], q4) + bs_ref[0, bias_idx]
        if valid is not None:
            s = jnp.where(valid, s, NEG)
        m_old = m_scr[...]
        m_new = jnp.maximum(m_old, jnp.max(s, axis=0, keepdims=True))
        alpha = jnp.exp(m_old - m_new)
        p = jnp.exp(s - m_new)
        l_scr[...] = alpha * l_scr[...] + jnp.sum(p, axis=0, keepdims=True)
        m_scr[...] = m_new
        p = p.astype(CDT)
        pv = jnp.concatenate([jnp.dot(vt_ref[0, 0, kb], p[:, r * qb:(r + 1) * qb], preferred_element_type=f32)
                              for r in range(HG_A)], axis=1)
        acc_scr[...] = alpha * acc_scr[...] + pv

    def reset():
        m_scr[...] = jnp.full(m_scr.shape, NEG, f32)
        l_scr[...] = jnp.zeros(l_scr.shape, f32)
        acc_scr[...] = jnp.zeros(acc_scr.shape, f32)

    def emit(o_ref):
        o = acc_scr[...] / l_scr[...]
        for r in range(HG_A):
            o_ref[0, :, r * DH:(r + 1) * DH] = o[:, r * qb:(r + 1) * qb].T

    def selected(kb):
        expand = jnp.where(m_col == 2 * kb + key_half, 1.0, 0.0).astype(CDT)
        return jnp.dot(expand, sel_c, preferred_element_type=f32) > 0.5

    reset()

    def far(kb, carry):
        step(kb, 2, ks_ref, vst_ref, selected(kb))
        return carry

    lax.fori_loop(0, jnp.maximum(qi - 1, 0), far, 0)

    @pl.when(qi > 0)
    def _():
        step(qi - 1, 1, ks_ref, vst_ref, selected(qi - 1))

    step(qi, 0, ks_ref, vst_ref, selected(qi) & (j_io <= i_io))
    emit(osel_ref)

    reset()
    n_back = WINDOW // qb
    for delta in range(n_back, 0, -1):
        @pl.when(qi >= delta)
        def _(delta=delta):
            step(qi - delta, 2 if delta > 1 else 1, kw_ref, vwt_ref, (j_io > i_io) if delta == n_back else None)

    step(qi, 0, kw_ref, vwt_ref, j_io <= i_io)
    emit(owin_ref)


def nsa_prompt(q_a, k_cmp, v_cmp, kv_s, kv_w, bias_c, bias_s):
    b, t, _ = q_a.shape
    qb = Q_BLOCK
    nb_c = t // L_CMP
    nq = t // qb
    perm = cmp_perm(nb_c)
    q = (q_a * SCALE).reshape(b, t, G_A, HG_A, DH).transpose(0, 2, 3, 1, 4).astype(CDT)
    kc = k_cmp[:, perm].transpose(0, 2, 1, 3).astype(CDT)
    vct = v_cmp[:, perm].transpose(0, 2, 3, 1).astype(CDT)
    ks = kv_s[:, :, 0].transpose(0, 2, 1, 3).astype(CDT)
    vst = kv_s[:, :, 1].reshape(b, nq, qb, G_A, DH).transpose(0, 3, 1, 4, 2).astype(CDT)
    kw = kv_w[:, :, 0].transpose(0, 2, 1, 3).astype(CDT)
    vwt = kv_w[:, :, 1].reshape(b, nq, qb, G_A, DH).transpose(0, 3, 1, 4, 2).astype(CDT)
    out = jax.ShapeDtypeStruct((b, t, H_A * DH), f32)
    ospec = pl.BlockSpec((1, qb, HG_A * DH), lambda bi, g, qi: (bi, qi, g))
    return pl.pallas_call(
        functools.partial(_nsa_body, nb_c=nb_c),
        grid=(b, G_A, nq),
        in_specs=[pl.BlockSpec((1, 1, HG_A, qb, DH), lambda bi, g, qi: (bi, g, 0, qi, 0)),
                  pl.BlockSpec((1, 1, nb_c, DH), lambda bi, g, qi: (bi, g, 0, 0)),
                  pl.BlockSpec((1, 1, DH, nb_c), lambda bi, g, qi: (bi, g, 0, 0)),
                  pl.BlockSpec((1, HG_A, 1, nb_c, qb), lambda bi, g, qi: (g, 0, qi, 0, 0)),
                  pl.BlockSpec((1, 1, t, DH), lambda bi, g, qi: (bi, g, 0, 0)),
                  pl.BlockSpec((1, 1, nq, DH, qb), lambda bi, g, qi: (bi, g, 0, 0, 0)),
                  pl.BlockSpec((1, 1, t, DH), lambda bi, g, qi: (bi, g, 0, 0)),
                  pl.BlockSpec((1, 1, nq, DH, qb), lambda bi, g, qi: (bi, g, 0, 0, 0)),
                  pl.BlockSpec((1, 3, qb, HG_A * qb), lambda bi, g, qi: (g, 0, 0, 0))],
        out_specs=[ospec, ospec, ospec],
        out_shape=[out, out, out],
        scratch_shapes=[pltpu.VMEM((1, HG_A * qb), f32), pltpu.VMEM((1, HG_A * qb), f32),
                        pltpu.VMEM((DH, HG_A * qb), f32)],
        compiler_params=pltpu.CompilerParams(dimension_semantics=("parallel", "parallel", "arbitrary"),
                                             vmem_limit_bytes=VMEM_LIMIT),
        name="nsa_prompt",
    )(q, kc, vct, bias_c, ks, vst, kw, vwt, bias_s)


NO_RANK = 99.0
PEER_CHUNK = 1024
PEER_JT = 32


def _extract_topk(work, iota_f, n_rows, top_ref=None):
    rank = jnp.full(work.shape, NO_RANK, f32)
    for k in range(PEER_TOPK):
        m = jnp.max(work, axis=0, keepdims=True)
        idx = jnp.min(jnp.where(work == m, iota_f, float(n_rows)), axis=0, keepdims=True)
        hit = iota_f == idx
        rank = jnp.where(hit, float(k), rank)
        work = jnp.where(hit, -jnp.inf, work)
        if top_ref is not None:
            top_ref[k:k + 1, :] = m
    return rank


def _peer_route_body(x_ref, g_ref, wq_ref, sk_ref, hb_ref, r2_ref, b_ref, c_ref, a_ref,
                     s_scr, top1_scr, top2_scr, cand_scr):
    x = x_ref[...]
    tn = x.shape[0]
    n_cand = PEER_TOPK * PEER_TOPK
    ms = jnp.mean(x * x, axis=-1, keepdims=True)
    hb = (x * lax.rsqrt(ms + EPS) * g_ref[...]).astype(CDT)
    hb_ref[...] = hb
    iota_k = lax.broadcasted_iota(jnp.int32, (N_KEYS, tn), 0).astype(f32)
    iota_c = lax.broadcasted_iota(jnp.int32, (n_cand, tn), 0).astype(f32)
    for p in range(PEER_HEADS):
        ranks = []
        for half in range(2):
            r0 = p * D_KEY + half * D_HALF
            q_t = _nt_dot(wq_ref[r0:r0 + D_HALF, :], hb)
            s_t = jnp.dot(sk_ref[p * 2 + half], q_t.astype(CDT), preferred_element_type=f32)
            s_scr[half] = s_t
            ranks.append(_extract_topk(s_t, iota_k, N_KEYS, top1_scr if half == 0 else top2_scr))
        top1 = top1_scr[...]
        top2 = top2_scr[...]
        for a in range(PEER_TOPK):
            cand_scr[a * PEER_TOPK:(a + 1) * PEER_TOPK, :] = top1[a:a + 1, :] + top2
        rank_c = _extract_topk(cand_scr[...], iota_c, n_cand)
        picked = jnp.where(rank_c < NO_RANK, 1.0, 0.0)
        e1 = jnp.exp(top1 - top1[0:1, :])
        e2 = jnp.exp(top2 - top2[0:1, :])
        z = jnp.zeros((1, tn), f32)
        c_tile = jnp.zeros((N_KEYS, tn), f32)
        for a in range(PEER_TOPK):
            pk = picked[a * PEER_TOPK:(a + 1) * PEER_TOPK, :]
            cnt = jnp.sum(pk, axis=0, keepdims=True)
            z = z + e1[a:a + 1, :] * jnp.sum(pk * e2, axis=0, keepdims=True)
            c_tile = jnp.where(ranks[0] == float(a), cnt, c_tile)
        r2_ref[p] = ranks[1]
        c_ref[p] = c_tile
        a_ref[p] = jnp.exp(s_scr[0] - top1[0:1, :])
        b_ref[p] = jnp.exp(s_scr[1] - top2[0:1, :]) / z


def peer_route(x2d, g, wq_t, sk, block_n=128):
    n, d = x2d.shape
    tile = jax.ShapeDtypeStruct((PEER_HEADS, N_KEYS, n), f32)
    tspec = pl.BlockSpec((PEER_HEADS, N_KEYS, block_n), lambda i: (0, 0, i))
    return pl.pallas_call(
        _peer_route_body,
        grid=(n // block_n,),
        in_specs=[pl.BlockSpec((block_n, d), lambda i: (i, 0)),
                  pl.BlockSpec((1, d), lambda i: (0, 0)),
                  pl.BlockSpec(wq_t.shape, lambda i: (0, 0)),
                  pl.BlockSpec(sk.shape, lambda i: (0, 0, 0))],
        out_specs=[pl.BlockSpec((block_n, d), lambda i: (i, 0)), tspec, tspec, tspec, tspec],
        out_shape=[jax.ShapeDtypeStruct((n, d), CDT), tile, tile, tile, tile],
        scratch_shapes=[pltpu.VMEM((2, N_KEYS, block_n), f32),
                        pltpu.VMEM((PEER_TOPK, block_n), f32),
                        pltpu.VMEM((PEER_TOPK, block_n), f32),
                        pltpu.VMEM((PEER_TOPK * PEER_TOPK, block_n), f32)],
        compiler_params=pltpu.CompilerParams(dimension_semantics=("parallel",), vmem_limit_bytes=VMEM_LIMIT),
        name="peer_route",
    )(x2d, g.reshape(1, d), wq_t, sk)


def _gelu_tanh(x):
    return 0.5 * x * (1.0 + jnp.tanh(math.sqrt(2.0 / math.pi) * (x + 0.044715 * (x * x * x))))


def _peer_dense_body(x_ref, hb_ref, r2_ref, b_ref, c_ref, a_ref, u_ref, vt_ref, o_ref,
                     acc_ref, ga_ref):
    ch = pl.program_id(1)
    n_i = PEER_CHUNK // N_KEYS

    @pl.when(ch == 0)
    def _():
        acc_ref[...] = jnp.zeros_like(acc_ref)

    hb = hb_ref[...]
    for i in range(n_i):
        gi = ch * n_i + i
        act = _nt_dot(u_ref[i * N_KEYS:(i + 1) * N_KEYS, :], hb)
        c_rows = [c_ref[p, pl.ds(gi, 1), :] for p in range(PEER_HEADS)]
        a_rows = [a_ref[p, pl.ds(gi, 1), :] for p in range(PEER_HEADS)]
        for jt in range(N_KEYS // PEER_JT):
            js = slice(jt * PEER_JT, (jt + 1) * PEER_JT)
            g = None
            for p in range(PEER_HEADS):
                term = jnp.where(r2_ref[p, js, :] < c_rows[p], b_ref[p, js, :], 0.0) * a_rows[p]
                g = term if g is None else g + term
            row0 = i * N_KEYS + jt * PEER_JT
            ga_ref[row0:row0 + PEER_JT, :] = (_gelu_tanh(act[js, :]) * g).astype(CDT)
    acc_ref[...] += jnp.dot(vt_ref[...], ga_ref[...], preferred_element_type=f32)

    @pl.when(ch == pl.num_programs(1) - 1)
    def _():
        o_ref[...] = x_ref[...] + acc_ref[...].T


def peer_dense(x2d, hb, r2, b, c, a, u, vt, block_n=256):
    n, d = x2d.shape
    n_ch = N_EXPERTS // PEER_CHUNK
    tspec = pl.BlockSpec((PEER_HEADS, N_KEYS, block_n), lambda i, k: (0, 0, i))
    return pl.pallas_call(
        _peer_dense_body,
        grid=(n // block_n, n_ch),
        in_specs=[pl.BlockSpec((block_n, d), lambda i, k: (i, 0)),
                  pl.BlockSpec((block_n, d), lambda i, k: (i, 0)),
                  tspec, tspec, tspec, tspec,
                  pl.BlockSpec((PEER_CHUNK, d), lambda i, k: (k, 0)),
                  pl.BlockSpec((d, PEER_CHUNK), lambda i, k: (0, k))],
        out_specs=pl.BlockSpec((block_n, d), lambda i, k: (i, 0)),
        out_shape=jax.ShapeDtypeStruct((n, d), f32),
        scratch_shapes=[pltpu.VMEM((d, block_n), f32),
                        pltpu.VMEM((PEER_CHUNK, block_n), CDT)],
        compiler_params=pltpu.CompilerParams(dimension_semantics=("parallel", "arbitrary"),
                                             vmem_limit_bytes=VMEM_LIMIT),
        name="peer_dense",
    )(x2d, hb, r2, b, c, a, u, vt)


def peer_residual(x, g, peer_w):
    wq_t, sk, u_c, vt_c = peer_w
    shp = x.shape
    x2d = x.reshape(-1, shp[-1])
    hb, r2, b, c, a = peer_route(x2d, g, wq_t, sk)
    return peer_dense(x2d, hb, r2, b, c, a, u_c, vt_c).reshape(shp)


FOX_HB = 4


def _fox_body(q_ref, k_ref, vt_ref, dq_ref, dk_ref, o_ref, m_scr, l_scr, acc_scr):
    qb = Q_BLOCK
    qi = pl.program_id(2)
    m_scr[...] = jnp.full(m_scr.shape, NEG, f32)
    l_scr[...] = jnp.zeros(l_scr.shape, f32)
    acc_scr[...] = jnp.zeros(acc_scr.shape, f32)
    j_io = lax.broadcasted_iota(jnp.int32, (qb, FOX_HB * qb), 0)
    i_io = lax.broadcasted_iota(jnp.int32, (qb, FOX_HB * qb), 1) % qb
    dq = dq_ref[0, 0]

    def step(kb, causal):
        k0 = pl.multiple_of(kb * qb, qb)
        s = jnp.concatenate([_nt_dot(k_ref[0, r, pl.ds(k0, qb), :], q_ref[0, r]) for r in range(FOX_HB)], axis=1)
        s = s + (dq - dk_ref[0, pl.ds(k0, qb), :])
        if causal:
            s = jnp.where(j_io <= i_io, s, NEG)
        m_old = m_scr[...]
        m_new = jnp.maximum(m_old, jnp.max(s, axis=0, keepdims=True))
        alpha = jnp.exp(m_old - m_new)
        p = jnp.exp(s - m_new)
        l_scr[...] = alpha * l_scr[...] + jnp.sum(p, axis=0, keepdims=True)
        m_scr[...] = m_new
        p = p.astype(CDT)
        pv = jnp.concatenate([jnp.dot(vt_ref[0, r, kb], p[:, r * qb:(r + 1) * qb], preferred_element_type=f32)
                              for r in range(FOX_HB)], axis=1)
        acc_scr[...] = alpha * acc_scr[...] + pv

    def far(kb, carry):
        step(kb, False)
        return carry

    lax.fori_loop(0, qi, far, 0)
    step(qi, True)
    o = acc_scr[...] / l_scr[...]
    for r in range(FOX_HB):
        o_ref[0, :, r * DH:(r + 1) * DH] = o[:, r * qb:(r + 1) * qb].T


def fox_prompt(q_b, k_b, v_b, dcum):
    b, t = q_b.shape[:2]
    qb = Q_BLOCK
    nq = t // qb
    q = (q_b * SCALE).transpose(0, 2, 1, 3).astype(CDT)
    k = k_b.transpose(0, 2, 1, 3).astype(CDT)
    vt = v_b.reshape(b, nq, qb, H_B, DH).transpose(0, 3, 1, 4, 2).astype(CDT)
    ng = H_B // FOX_HB
    d_t = dcum.transpose(0, 2, 1)
    dq = d_t.reshape(b, ng, FOX_HB, nq, qb).transpose(0, 1, 3, 2, 4).reshape(b, ng, nq, 1, FOX_HB * qb)
    dk = jnp.broadcast_to(d_t.reshape(b, ng, FOX_HB, t, 1), (b, ng, FOX_HB, t, qb)).transpose(0, 1, 3, 2, 4)
    dk = dk.reshape(b * ng, t, FOX_HB * qb)
    return pl.pallas_call(
        _fox_body,
        grid=(b, ng, nq),
        in_specs=[pl.BlockSpec((1, FOX_HB, qb, DH), lambda bi, g, qi: (bi, g, qi, 0)),
                  pl.BlockSpec((1, FOX_HB, t, DH), lambda bi, g, qi: (bi, g, 0, 0)),
                  pl.BlockSpec((1, FOX_HB, nq, DH, qb), lambda bi, g, qi: (bi, g, 0, 0, 0)),
                  pl.BlockSpec((1, 1, None, 1, FOX_HB * qb), lambda bi, g, qi: (bi, g, qi, 0, 0)),
                  pl.BlockSpec((1, t, FOX_HB * qb), lambda bi, g, qi: (bi * ng + g, 0, 0))],
        out_specs=pl.BlockSpec((1, qb, FOX_HB * DH), lambda bi, g, qi: (bi, qi, g)),
        out_shape=jax.ShapeDtypeStruct((b, t, H_B * DH), f32),
        scratch_shapes=[pltpu.VMEM((1, FOX_HB * qb), f32), pltpu.VMEM((1, FOX_HB * qb), f32),
                        pltpu.VMEM((DH, FOX_HB * qb), f32)],
        compiler_params=pltpu.CompilerParams(dimension_semantics=("parallel", "parallel", "arbitrary"),
                                             vmem_limit_bytes=VMEM_LIMIT),
        name="fox_prompt",
    )(q, k, vt, dq, dk)


ROWS = 64
SEL_COLS = 40


def _online_update(s, v, m_ref, l_ref, acc_ref):
    m_old = m_ref[...]
    m_new = jnp.maximum(m_old, jnp.max(s, axis=-1, keepdims=True))
    alpha = jnp.exp(m_old - m_new)
    p = jnp.exp(s - m_new)
    l_ref[...] = alpha * l_ref[...] + jnp.sum(p, axis=-1, keepdims=True)
    m_ref[...] = m_new
    acc_ref[...] = alpha * acc_ref[...] + jnp.dot(p.astype(CDT), v, preferred_element_type=f32)


def _sample_body(pt_ref, qa_ref, qb_ref, mrow_ref, selc_ref, seln_ref, bsel_ref, foxc_ref, foxn_ref, fb_ref,
                 win_ref, winn_ref, bwin_ref, osel_ref, ofox_ref, owin_ref,
                 ms_ref, ls_ref, as_ref, mf_ref, lf_ref, af_ref, *, past):
    pg = pl.program_id(1)
    n_pg = pl.num_programs(1)
    ps = PAGE_SIZE
    w_a = G_A * DH
    w_b = H_B * DH
    ts = ROWS // H_A
    i_row = lax.broadcasted_iota(jnp.int32, (ROWS, ps), 0) % ts
    j_col = lax.broadcasted_iota(jnp.int32, (ROWS, ps), 1)
    qa = qa_ref[0]
    qb = qb_ref[0]

    @pl.when(pg == 0)
    def _():
        ms_ref[...] = jnp.full(ms_ref.shape, NEG, f32)
        ls_ref[...] = jnp.zeros(ls_ref.shape, f32)
        as_ref[...] = jnp.zeros(as_ref.shape, f32)
        mf_ref[...] = jnp.full(mf_ref.shape, NEG, f32)
        lf_ref[...] = jnp.zeros(lf_ref.shape, f32)
        af_ref[...] = jnp.zeros(af_ref.shape, f32)
        kw = win_ref[0, 0, :, 0:w_a].astype(CDT)
        vw = win_ref[0, 0, :, w_a:2 * w_a].astype(CDT)
        s1 = _nt_dot(qa, kw) + bwin_ref[:, 0:WINDOW]
        iw = lax.broadcasted_iota(jnp.int32, (ROWS, WINDOW), 0) % ts
        jw = lax.broadcasted_iota(jnp.int32, (ROWS, WINDOW), 1)
        s1 = jnp.where(jw > iw, s1, NEG)
        kn = winn_ref[0, :, 0:w_a].astype(CDT)
        vn = winn_ref[0, :, w_a:2 * w_a].astype(CDT)
        s2 = _nt_dot(qa, kn) + bwin_ref[:, WINDOW:WINDOW + ps]
        s2 = jnp.where(j_col <= i_row, s2, NEG)
        m = jnp.maximum(jnp.max(s1, axis=-1, keepdims=True), jnp.max(s2, axis=-1, keepdims=True))
        p1 = jnp.exp(s1 - m)
        p2 = jnp.exp(s2 - m)
        l = jnp.sum(p1, axis=-1, keepdims=True) + jnp.sum(p2, axis=-1, keepdims=True)
        o = (jnp.dot(p1.astype(CDT), vw, preferred_element_type=f32)
             + jnp.dot(p2.astype(CDT), vn, preferred_element_type=f32))
        owin_ref[0] = o / l

    causal = (pg * ps + j_col) <= (past + i_row)

    def sel_step(blk_ref):
        k = blk_ref[0, :, 0:w_a].astype(CDT)
        v = blk_ref[0, :, w_a:2 * w_a].astype(CDT)
        s = _nt_dot(qa, k) + bsel_ref[0]
        blk_of_key = 2 * pg + lax.broadcasted_iota(jnp.int32, (SEL_COLS, ps), 1) // L_SEL
        expand = jnp.where(lax.broadcasted_iota(jnp.int32, (SEL_COLS, ps), 0) == blk_of_key, 1.0, 0.0).astype(CDT)
        picked = jnp.dot(mrow_ref[0].astype(CDT), expand, preferred_element_type=f32) > 0.5
        s = jnp.where(picked & causal, s, NEG)
        _online_update(s, v, ms_ref, ls_ref, as_ref)

    def fox_step(blk_ref):
        k = blk_ref[0, :, 0:w_b].astype(CDT)
        v = blk_ref[0, :, w_b:2 * w_b].astype(CDT)
        s = _nt_dot(qb, k) + fb_ref[0, 0]
        s = jnp.where(causal, s, NEG)
        _online_update(s, v, mf_ref, lf_ref, af_ref)

    @pl.when(pg < n_pg - 1)
    def _():
        sel_step(selc_ref)
        fox_step(foxc_ref)

    @pl.when(pg == n_pg - 1)
    def _():
        sel_step(seln_ref)
        fox_step(foxn_ref)
        osel_ref[0] = as_ref[...] / ls_ref[...]
        ofox_ref[0] = af_ref[...] / lf_ref[...]


def sample_attention(pt_l, qa_bd, qb_bd, mrows, sel_cache2, sel_new, bias_sel, fox_cache2, fox_new, fbias,
                     win_state, win_new, bias_win, *, layer, past):
    bs, n_pages = pt_l.shape
    ps = PAGE_SIZE
    n_pg = n_pages + 1
    w_a = G_A * DH
    w_b = H_B * DH
    last = n_pages - 1
    grid_spec = pltpu.PrefetchScalarGridSpec(
        num_scalar_prefetch=1,
        grid=(bs, n_pg),
        in_specs=[
            pl.BlockSpec((1, ROWS, w_a), lambda b, p, pt: (b, 0, 0)),
            pl.BlockSpec((1, ROWS, w_b), lambda b, p, pt: (b, 0, 0)),
            pl.BlockSpec((1, ROWS, SEL_COLS), lambda b, p, pt: (b, 0, 0)),
            pl.BlockSpec((1, ps, 2 * w_a), lambda b, p, pt: (pt[b, jnp.minimum(p, last)], 0, 0)),
            pl.BlockSpec((1, ps, 2 * w_a), lambda b, p, pt: (b, 0, 0)),
            pl.BlockSpec((1, ROWS, ps), lambda b, p, pt: (p, 0, 0)),
            pl.BlockSpec((1, ps, 2 * w_b), lambda b, p, pt: (pt[b, jnp.minimum(p, last)], 0, 0)),
            pl.BlockSpec((1, ps, 2 * w_b), lambda b, p, pt: (b, 0, 0)),
            pl.BlockSpec((1, 1, ROWS, ps), lambda b, p, pt: (b, p, 0, 0)),
            pl.BlockSpec((1, 1, WINDOW, 2 * w_a), lambda b, p, pt: (layer, b, 0, 0)),
            pl.BlockSpec((1, ps, 2 * w_a), lambda b, p, pt: (b, 0, 0)),
            pl.BlockSpec((ROWS, WINDOW + ps), lambda b, p, pt: (0, 0)),
        ],
        out_specs=[pl.BlockSpec((1, ROWS, w_a), lambda b, p, pt: (b, 0, 0)),
                   pl.BlockSpec((1, ROWS, w_b), lambda b, p, pt: (b, 0, 0)),
                   pl.BlockSpec((1, ROWS, w_a), lambda b, p, pt: (b, 0, 0))],
        scratch_shapes=[pltpu.VMEM((ROWS, 1), f32), pltpu.VMEM((ROWS, 1), f32), pltpu.VMEM((ROWS, w_a), f32),
                        pltpu.VMEM((ROWS, 1), f32), pltpu.VMEM((ROWS, 1), f32), pltpu.VMEM((ROWS, w_b), f32)],
    )
    return pl.pallas_call(
        functools.partial(_sample_body, past=past),
        grid_spec=grid_spec,
        out_shape=[jax.ShapeDtypeStruct((bs, ROWS, w_a), f32), jax.ShapeDtypeStruct((bs, ROWS, w_b), f32),
                   jax.ShapeDtypeStruct((bs, ROWS, w_a), f32)],
        compiler_params=pltpu.CompilerParams(dimension_semantics=("parallel", "arbitrary"),
                                             vmem_limit_bytes=VMEM_LIMIT),
        name="sample_attention",
    )(pt_l, qa_bd, qb_bd, mrows, sel_cache2, sel_new, bias_sel, fox_cache2, fox_new, fbias, win_state, win_new,
      bias_win)


def sample_bias_tables(table, past, ts, n_pg):
    tg = table.reshape(N_BUCKETS, H_A)
    tq = past + np.arange(ts)
    s_abs = np.arange(n_pg * PAGE_SIZE)
    bsel = t5_bucket_np(tq[:, None] - s_abs[None, :])
    k_pos_w = np.concatenate([past - WINDOW + np.arange(WINDOW), past + np.arange(PAGE_SIZE)])
    bwin = t5_bucket_np(tq[:, None] - k_pos_w[None, :])

    def rows(bk):
        return jnp.transpose(tg[bk], (2, 0, 1)).reshape(H_A * ts, bk.shape[1])

    bias_sel = rows(bsel).reshape(ROWS, n_pg, PAGE_SIZE).transpose(1, 0, 2)
    return bias_sel.astype(f32), rows(bwin).astype(f32)


def block_diag_rows(q, heads_per_group):
    bs, ts, h, dh = q.shape
    n_c = h // heads_per_group
    qh = q.transpose(0, 2, 1, 3)
    onehot = (np.arange(h)[:, None] // heads_per_group == np.arange(n_c)[None, :]).astype(np.float32)
    out = qh[:, :, :, None, :] * onehot[None, :, None, :, None]
    return out.reshape(bs, h * ts, n_c * dh)


def diag_extract(o, heads_per_group, ts):
    bs = o.shape[0]
    h = o.shape[1] // ts
    n_c = h // heads_per_group
    own = (np.arange(h)[:, None] // heads_per_group == np.arange(n_c)[None, :]).astype(np.float32)
    o5 = o.reshape(bs, h, ts, n_c, DH) * own[None, :, None, :, None]
    return o5.sum(axis=3).transpose(0, 2, 1, 3)


def sample_cmp_select(q_a, q_pos, k_cmp, v_cmp, table):
    b, nb_c = k_cmp.shape[:2]
    nb_s = nb_c * L_CMP // L_SEL
    n_sel = min(N_SEL, nb_s)
    end_pos = jnp.arange(nb_c) * L_CMP + (L_CMP - 1)
    tb = q_pos.shape[0]
    qg = q_a.reshape(b, tb, G_A, HG_A, DH)
    dist_c = q_pos[:, None] - end_pos[None, :]
    mask_c = dist_c >= 0
    s = jnp.einsum('btgrd,bngd->bgrtn', qg, k_cmp).astype(f32) * SCALE + rel_bias_grid(dist_c, table)
    p = jnp.where(mask_c, jax.nn.softmax(jnp.where(mask_c, s, NEG), axis=-1), 0.0)
    o_cmp = jnp.einsum('bgrtn,bngd->btgrd', p.astype(v_cmp.dtype), v_cmp)
    imp = p.sum(2).reshape(b, G_A, tb, nb_s, L_SEL // L_CMP).sum(-1)
    cur = (q_pos // L_SEL)[:, None]
    bidx = jnp.arange(nb_s)[None, :]
    imp = jnp.where((bidx == 0) | (bidx == cur), FORCE, imp)
    imp = jnp.where(bidx <= cur, imp, NEG)
    _, sel = lax.top_k(imp, n_sel)
    return o_cmp.reshape(b, tb, H_A, DH), sel


def sample_group_attention(l, q_a, kv_s, kv_w, q_b, k_b, v_b, logf, sel, cache_sel, cache_fox, cache_logf,
                           win_state, page_table, table):
    bs, ts = q_a.shape[:2]
    n_pages = page_table.shape[1]
    past = n_pages * PAGE_SIZE
    n_pool = cache_sel.shape[1]
    ps = PAGE_SIZE
    pt_l = (page_table + l * n_pool).astype(jnp.int32)
    qa_bd = block_diag_rows(q_a * SCALE, HG_A).astype(CDT)
    qb_bd = block_diag_rows(q_b * SCALE, 1).astype(CDT)
    onehot = jnp.sum(jax.nn.one_hot(sel, SEL_COLS, dtype=f32), axis=-2)
    mrows = jnp.broadcast_to(onehot[:, :, None], (bs, G_A, HG_A, ts, SEL_COLS)).reshape(bs, ROWS, SEL_COLS)
    padn = lambda a: jnp.pad(a.reshape(bs, ts, -1), ((0, 0), (0, ps - ts), (0, 0)))
    sel_new = padn(kv_s)
    win_new = padn(kv_w)
    fox_new = padn(jnp.stack([k_b, v_b], axis=2))
    bias_sel, bias_win = sample_bias_tables(table, past, ts, n_pages + 1)
    lf_past = cache_logf.reshape(cache_logf.shape[0], n_pool, ps * H_B)[l][page_table].reshape(bs, past, H_B)
    dk = jnp.cumsum(jnp.concatenate([lf_past, logf], axis=1), axis=1)
    dk_pad = jnp.pad(dk, ((0, 0), (0, ps - ts), (0, 0)))
    dq = dk[:, past:]
    fbias = dq.transpose(0, 2, 1)[:, :, :, None] - dk_pad.transpose(0, 2, 1)[:, :, None, :]
    fbias = fbias.reshape(bs, ROWS, n_pages + 1, ps).transpose(0, 2, 1, 3)
    sel_cache2 = cache_sel.reshape(-1, ps, 2 * G_A * DH)
    fox_cache2 = cache_fox.reshape(-1, ps, 2 * H_B * DH)
    win2 = win_state.reshape(win_state.shape[0], bs, WINDOW, 2 * G_A * DH)
    o_sel, o_fox, o_win = sample_attention(pt_l, qa_bd, qb_bd, mrows, sel_cache2, sel_new, bias_sel, fox_cache2,
                                           fox_new, fbias, win2, win_new, bias_win, layer=l, past=past)
    return diag_extract(o_sel, HG_A, ts), diag_extract(o_win, HG_A, ts), diag_extract(o_fox, 1, ts)


def _sigmoid(x):
    return 1.0 / (1.0 + jnp.exp(-x))

def _mix_body(x_ref, oc_ref, os_ref, ow_ref, gn_ref, ob_ref, ga_ref, gb_ref, wpa_ref, wpb_ref, wo_ref, o_ref):
    gn = _sigmoid(gn_ref[...])
    o_a = gn[:, 0:W_A] * oc_ref[...] + gn[:, W_A:2 * W_A] * os_ref[...] + gn[:, 2 * W_A:3 * W_A] * ow_ref[...]
    pa = jnp.dot(o_a.astype(CDT), wpa_ref[...], preferred_element_type=f32)
    pb = jnp.dot(ob_ref[...].astype(CDT), wpb_ref[...], preferred_element_type=f32)
    mix = _sigmoid(ga_ref[...]) * pa + _sigmoid(gb_ref[...]) * pb
    o_ref[...] = x_ref[...] + jnp.dot(mix.astype(CDT), wo_ref[...], preferred_element_type=f32)

def mix_residual(x2d, o_cmp, o_sel, o_win, gn_exp, o_b, gate_a, gate_b, w_pa, w_pb, w_o, block_n=256):
    n, d = x2d.shape
    row = lambda w: pl.BlockSpec((block_n, w), lambda i: (i, 0))
    full = lambda a: pl.BlockSpec(a.shape, lambda i: (0, 0))
    return pl.pallas_call(
        _mix_body,
        grid=(n // block_n,),
        in_specs=[row(d), row(W_A), row(W_A), row(W_A), row(3 * W_A), row(W_B), row(d), row(d),
                  full(w_pa), full(w_pb), full(w_o)],
        out_specs=row(d),
        out_shape=jax.ShapeDtypeStruct((n, d), f32),
        compiler_params=pltpu.CompilerParams(dimension_semantics=("parallel",), vmem_limit_bytes=VMEM_LIMIT),
        name="mix_residual",
    )(x2d, o_cmp, o_sel, o_win, gn_exp, o_b, gate_a, gate_b, w_pa, w_pb, w_o)

def _ple_body(x_ref, p_ref, g_ref, wpe_ref, wpg_ref, gf_ref, o_ref, y_ref):
    x = x_ref[...]
    ms = jnp.mean(x * x, axis=-1, keepdims=True)
    h = x * lax.rsqrt(ms + EPS) * g_ref[...]
    gate = _sigmoid(jnp.dot(h.astype(CDT), wpg_ref[...], preferred_element_type=f32))
    xo = x + jnp.dot(p_ref[...].astype(CDT), wpe_ref[...], preferred_element_type=f32) * gate
    o_ref[...] = xo
    ms2 = jnp.mean(xo * xo, axis=-1, keepdims=True)
    y_ref[...] = xo * lax.rsqrt(ms2 + EPS) * gf_ref[...]

def ple_residual(x2d, p2d, g_p, w_pe, w_pg, g_final, block_n=256):
    n, d = x2d.shape
    row = lambda w: pl.BlockSpec((block_n, w), lambda i: (i, 0))
    full = lambda a: pl.BlockSpec(a.shape, lambda i: (0, 0))
    out = jax.ShapeDtypeStruct((n, d), f32)
    return pl.pallas_call(
        _ple_body,
        grid=(n // block_n,),
        in_specs=[row(d), row(p2d.shape[1]), pl.BlockSpec((1, d), lambda i: (0, 0)), full(w_pe), full(w_pg),
                  pl.BlockSpec((1, d), lambda i: (0, 0))],
        out_specs=[row(d), row(d)],
        out_shape=[out, out],
        compiler_params=pltpu.CompilerParams(dimension_semantics=("parallel",), vmem_limit_bytes=VMEM_LIMIT),
        name="ple_residual",
    )(x2d, p2d, g_p.reshape(1, d), w_pe, w_pg, g_final.reshape(1, d))


def rms_norm(x, g):
    x32 = x.astype(jnp.float32)
    y = x32 * lax.rsqrt(jnp.mean(x32 * x32, axis=-1, keepdims=True) + EPS)
    return (y * g.astype(jnp.float32)).astype(x.dtype)


def t5_bucket(dist):
    n = jnp.maximum(dist, 0)
    max_exact = N_BUCKETS // 2
    nf = jnp.maximum(n, 1).astype(jnp.float32)
    large = max_exact + (jnp.log(nf / max_exact) / math.log(MAX_DISTANCE / max_exact)
                         * (N_BUCKETS - max_exact)).astype(jnp.int32)
    large = jnp.minimum(large, N_BUCKETS - 1)
    return jnp.where(n < max_exact, n, large)


def rel_bias_grid(dist, table):
    b = table[t5_bucket(dist)].astype(jnp.float32)
    return jnp.moveaxis(b.reshape(*dist.shape, G_A, HG_A), (-2, -1), (0, 1))


def over_query_blocks(fn, q_pos, *qs):
    tq = q_pos.shape[0]
    if tq <= Q_BLOCK or tq % Q_BLOCK:
        return fn(q_pos, *qs)
    nb = tq // Q_BLOCK
    split = lambda a: jnp.moveaxis(a.reshape(a.shape[0], nb, Q_BLOCK, *a.shape[2:]), 1, 0)
    out = lax.map(lambda args: fn(*args), (q_pos.reshape(nb, Q_BLOCK),) + tuple(split(a) for a in qs))
    merge = lambda a: jnp.moveaxis(a, 0, 1).reshape(a.shape[1], tq, *a.shape[3:])
    return jax.tree_util.tree_map(merge, out)


def masked_attend(q, k, v, bias, mask):
    s = jnp.einsum('btgrd,bsgd->bgrts', q, k).astype(jnp.float32) * SCALE + bias
    p = jnp.where(mask, jax.nn.softmax(jnp.where(mask, s, NEG), axis=-1), 0.0)
    return jnp.einsum('bgrts,bsgd->btgrd', p.astype(v.dtype), v)


def window_attend(q, q_pos, k, v, k_pos, table):
    b, tq = q.shape[:2]
    dist = q_pos[:, None] - k_pos[None, :]
    mask = (dist >= 0) & (dist < WINDOW) & (k_pos[None, :] >= 0)
    o = masked_attend(q.reshape(b, tq, G_A, HG_A, DH), k, v, rel_bias_grid(dist, table), mask)
    return o.reshape(b, tq, H_A, DH)


def window_prompt(q, kw, vw, table):
    t = q.shape[1]
    padf = lambda a: jnp.pad(a, ((0, 0), (WINDOW, 0), (0, 0), (0, 0)))
    kp, vp = padf(kw), padf(vw)

    def blk(q_pos, qb):
        span = WINDOW + q_pos.shape[0]
        start = q_pos[0]
        kb = lax.dynamic_slice_in_dim(kp, start, span, axis=1)
        vb = lax.dynamic_slice_in_dim(vp, start, span, axis=1)
        k_pos = start - WINDOW + jnp.arange(span)
        return window_attend(qb, q_pos, kb, vb, k_pos, table)

    return over_query_blocks(blk, jnp.arange(t), q)


def nsa_cmp_sel(q, q_pos, kc, vc, ks, vs, w_ck, w_cv, pe, table):
    b, tk = kc.shape[:2]
    pad = (-tk) % L_SEL
    padt = lambda a: jnp.pad(a, ((0, 0), (0, pad), (0, 0), (0, 0)))
    kc, vc, ks, vs = padt(kc), padt(vc), padt(ks), padt(vs)
    tp = tk + pad
    nb_c, nb_s = tp // L_CMP, tp // L_SEL
    n_sel = min(N_SEL, nb_s)
    blocks = lambda a: a.reshape(b, nb_c, L_CMP, G_A, DH) + pe[:, None, :].astype(a.dtype)
    k_cmp = jnp.einsum('bnlgd,lde->bnge', blocks(kc), w_ck)
    v_cmp = jnp.einsum('bnlgd,lde->bnge', blocks(vc), w_cv)
    end_pos = jnp.arange(nb_c) * L_CMP + (L_CMP - 1)
    ks_b = jnp.moveaxis(ks.reshape(b, nb_s, L_SEL, G_A, DH), 3, 1)
    vs_b = jnp.moveaxis(vs.reshape(b, nb_s, L_SEL, G_A, DH), 3, 1)
    b_ix = jnp.arange(b)[:, None, None, None]
    g_ix = jnp.arange(G_A)[None, :, None, None]
    table_g = table.reshape(N_BUCKETS, G_A, HG_A)

    def blk(q_pos, qb):
        tb = q_pos.shape[0]
        qg = qb.reshape(b, tb, G_A, HG_A, DH)
        dist_c = q_pos[:, None] - end_pos[None, :]
        mask_c = dist_c >= 0
        s = jnp.einsum('btgrd,bngd->bgrtn', qg, k_cmp).astype(jnp.float32) * SCALE + rel_bias_grid(dist_c, table)
        p = jnp.where(mask_c, jax.nn.softmax(jnp.where(mask_c, s, NEG), axis=-1), 0.0)
        o_cmp = jnp.einsum('bgrtn,bngd->btgrd', p.astype(v_cmp.dtype), v_cmp)
        imp = p.sum(2).reshape(b, G_A, tb, nb_s, L_SEL // L_CMP).sum(-1)
        cur = (q_pos // L_SEL)[:, None]
        bidx = jnp.arange(nb_s)[None, :]
        imp = jnp.where((bidx == 0) | (bidx == cur), FORCE, imp)
        imp = jnp.where(bidx <= cur, imp, NEG)
        _, sel = lax.top_k(imp, n_sel)
        kg = ks_b[b_ix, g_ix, sel]
        vg = vs_b[b_ix, g_ix, sel]
        tok = sel[..., None] * L_SEL + jnp.arange(L_SEL)
        dist_s = q_pos[None, None, :, None, None] - tok
        bias_s = jnp.moveaxis(table_g[t5_bucket(dist_s), g_ix[..., None]], -1, 2).astype(jnp.float32)
        mask_s = (dist_s >= 0)[:, :, None]
        s2 = jnp.einsum('btgrd,bgtnld->bgrtnl', qg, kg).astype(jnp.float32) * SCALE + bias_s
        s2 = jnp.where(mask_s, s2, NEG).reshape(b, G_A, HG_A, tb, n_sel * L_SEL)
        p2 = jax.nn.softmax(s2, axis=-1)
        o_sel = jnp.einsum('bgrtk,bgtkd->btgrd', p2.astype(vg.dtype), vg.reshape(b, G_A, tb, n_sel * L_SEL, DH))
        return (o_cmp.reshape(b, tb, H_A, DH), o_sel.reshape(b, tb, H_A, DH))

    return over_query_blocks(blk, q_pos, q)


def fox_attend(q, dq, q_pos, k, v, dk, k_pos):
    b = q.shape[0]
    dk_t = jnp.moveaxis(dk, 2, 1)[:, :, None, None, :]

    def blk(q_pos, qb, dqb):
        tb = q_pos.shape[0]
        bias = jnp.moveaxis(dqb, 2, 1)[:, :, None, :, None] - dk_t
        mask = k_pos[None, :] <= q_pos[:, None]
        o = masked_attend(qb.reshape(b, tb, H_B, 1, DH), k, v, bias, mask)
        return o.reshape(b, tb, H_B, DH)

    return over_query_blocks(blk, q_pos, q, dq)


def project(x, g, w_in_pad):
    b, t = x.shape[:2]
    z = norm_proj(x.reshape(b * t, D_MODEL), g, w_in_pad)[:, :N_IN].reshape(b, t, N_IN)
    (q_a, kv_c, kv_s, kv_w, g_nsa, q_b, k_b, v_b, f_b, gate_a, gate_b) = jnp.split(
        z, np.cumsum(SPLIT_SIZES)[:-1].tolist(), axis=-1)
    kv = lambda a: a.reshape(b, t, 2, G_A, DH)
    hd = lambda a, nh: a.reshape(b, t, nh, DH)
    return (hd(q_a, H_A), kv(kv_c), kv(kv_s), kv(kv_w), g_nsa.reshape(b, t, 3, H_A),
            hd(q_b, H_B), hd(k_b, H_B), hd(v_b, H_B), f_b, gate_a, gate_b)


def layer_tail(x, o_cmp, o_sel, o_win, g_nsa, o_b, gate_a, gate_b, p_l, lw, peer_w, g_final):
    (w_pa, w_pb, w_o, g_f, g_p, w_pe, w_pg) = lw
    b, t, d = x.shape
    n = b * t
    gn_exp = jnp.repeat(g_nsa.reshape(n, 3 * H_A), DH, axis=1)
    x1 = mix_residual(x.reshape(n, d), o_cmp.reshape(n, W_A), o_sel.reshape(n, W_A), o_win.reshape(n, W_A), gn_exp,
                      o_b.reshape(n, W_B), gate_a.reshape(n, d), gate_b.reshape(n, d), w_pa, w_pb, w_o)
    x2 = peer_residual(x1, g_f, peer_w)
    x3, y = ple_residual(x2, p_l.reshape(n, -1), g_p, w_pe, w_pg, g_final)
    return x3.reshape(b, t, d), y.reshape(b, t, d)


def kernel(x_prompt, x_sample, cache_nsa_cmp_kv, cache_nsa_sel_kv, cache_fox_kv, cache_fox_logf,
           state_nsa_win_kv, page_table, p_prompt, p_sample, g_mix, w_in, b_forget, w_cmp_k, w_cmp_v,
           pe_cmp, rel_bias_table, w_proj_a, w_proj_b, w_out, g_ffn, peer_w_q, peer_sub_keys,
           peer_u, peer_v, g_ple, w_ple, w_ple_gate, g_final):
    depth = w_in.shape[0]
    bp, tp = x_prompt.shape[:2]
    bs, ts = x_sample.shape[:2]
    past = page_table.shape[1] * PAGE_SIZE
    win_p = min(WINDOW, tp)
    win_buf = state_nsa_win_kv.shape[2]
    pos_p = jnp.arange(tp)
    pos_s = past + jnp.arange(ts)
    pos_all = jnp.arange(past + ts)
    pos_w = past - win_buf + jnp.arange(win_buf + ts)
    n_in_pad = -(-N_IN // LANES) * LANES
    w_in_pad = jnp.pad(w_in, ((0, 0), (0, 0), (0, n_in_pad - N_IN))).astype(CDT)
    bias_c, bias_s = nsa_bias_tables(rel_bias_table, tp)
    xp, xs = x_prompt, x_sample
    pc, psel, pw, pfk, pfl = [], [], [], [], []
    sc, ssel, sw, sfk, sfl = [], [], [], [], []
    for l in range(depth):
        lw = (w_proj_a[l].astype(CDT), w_proj_b[l].astype(CDT), w_out[l].astype(CDT), g_ffn[l], g_ple[l],
              w_ple[l].astype(CDT), w_ple_gate[l].astype(CDT))
        peer_w = (peer_w_q[l].T.astype(CDT),
                  peer_sub_keys[l].reshape(PEER_HEADS * 2, N_KEYS, D_HALF).astype(CDT),
                  peer_u[l].astype(CDT), peer_v[l].T.astype(CDT))
        cmp_w = (w_cmp_k[l], w_cmp_v[l], pe_cmp[l], rel_bias_table)

        (q_a, kv_c, kv_s, kv_w, g_nsa, q_b, k_b, v_b, f_b, gate_a, gate_b) = project(xp, g_mix[l], w_in_pad[l])
        k_cmp, v_cmp = nsa_compress(kv_c, w_cmp_k[l], w_cmp_v[l], pe_cmp[l])
        o_cmp, o_sel, o_win = nsa_prompt(q_a.reshape(bp, tp, W_A), k_cmp, v_cmp, kv_s, kv_w, bias_c, bias_s)
        o_cmp = o_cmp.reshape(bp, tp, H_A, DH)
        o_sel = o_sel.reshape(bp, tp, H_A, DH)
        o_win = o_win.reshape(bp, tp, H_A, DH)
        logf = jax.nn.log_sigmoid(f_b.astype(f32) + b_forget[l].astype(f32))
        dcum = jnp.cumsum(logf, axis=1)
        o_b = fox_prompt(q_b, k_b, v_b, dcum).reshape(bp, tp, H_B, DH)
        xp, y_prompt = layer_tail(xp, o_cmp, o_sel, o_win, g_nsa, o_b, gate_a, gate_b, p_prompt[l], lw, peer_w, g_final)
        pc.append(kv_c)
        psel.append(kv_s)
        pw.append(kv_w[:, tp - win_p:])
        pfk.append(jnp.stack([k_b, v_b], axis=2))
        pfl.append(logf)

        (q_a, kv_c, kv_s, kv_w, g_nsa, q_b, k_b, v_b, f_b, gate_a, gate_b) = project(xs, g_mix[l], w_in_pad[l])
        kc_pool, vc_pool = nsa_compress(cache_nsa_cmp_kv[l], *cmp_w[:3])
        kc_new, vc_new = nsa_compress(jnp.pad(kv_c, ((0, 0), (0, L_SEL - ts), (0, 0), (0, 0), (0, 0))), *cmp_w[:3])
        gather_pages = lambda a: a[page_table].reshape(bs, past // L_CMP, G_A, DH)
        k_cmp = jnp.concatenate([gather_pages(kc_pool), kc_new], axis=1)
        v_cmp = jnp.concatenate([gather_pages(vc_pool), vc_new], axis=1)
        o_cmp, sel = sample_cmp_select(q_a, pos_s, k_cmp, v_cmp, rel_bias_table)
        w_all = jnp.concatenate([state_nsa_win_kv[l], kv_w], axis=1)
        logf = jax.nn.log_sigmoid(f_b.astype(f32) + b_forget[l].astype(f32))
        o_sel, o_win, o_b = sample_group_attention(l, q_a, kv_s, kv_w, q_b, k_b, v_b, logf, sel, cache_nsa_sel_kv,
                                                   cache_fox_kv, cache_fox_logf, state_nsa_win_kv, page_table,
                                                   rel_bias_table)
        xs, y_sample = layer_tail(xs, o_cmp, o_sel, o_win, g_nsa, o_b, gate_a, gate_b, p_sample[l], lw, peer_w, g_final)
        sc.append(kv_c)
        ssel.append(kv_s)
        sw.append(w_all[:, ts:])
        sfk.append(jnp.stack([k_b, v_b], axis=2))
        sfl.append(logf)

    return (y_prompt, y_sample,
            jnp.stack(pc), jnp.stack(psel), jnp.stack(pw), jnp.stack(pfk), jnp.stack(pfl),
            jnp.stack(sc), jnp.stack(ssel), jnp.stack(sw), jnp.stack(sfk), jnp.stack(sfl))
```
